```python
import math
import jax
import jax.numpy as jnp
from jax import lax
import numpy as np

D_MODEL = 1024
BATCH = 2
SEQ = 8192
DEPTH = 4
DEC_BATCH = 128
DEC_SEQ = 4
PAST_LEN = 2048
PAGE_SIZE = 128

HEAD_DIM = 64
N_MIXERS = 3
N_FOX = (DEPTH + 2) // N_MIXERS
N_DIL = (DEPTH + 1) // N_MIXERS
N_MOBA = DEPTH // N_MIXERS
FOX_HEADS = D_MODEL // HEAD_DIM
FOX_WIDTH = FOX_HEADS * HEAD_DIM
FOX_BIAS_INIT = 3.0
Q_BLOCK = 128
DIL_PATTERNS = ((128, 1), (512, 4), (2048, 16))
DIL_GROUPS = len(DIL_PATTERNS)
DIL_GROUP_HEADS = D_MODEL // (2 * HEAD_DIM)
DIL_HEADS = DIL_GROUPS * DIL_GROUP_HEADS
DIL_QKV = DIL_HEADS * HEAD_DIM
DIL_WIDTH = DIL_GROUP_HEADS * HEAD_DIM
MOBA_HEADS = D_MODEL // HEAD_DIM
MOBA_WIDTH = MOBA_HEADS * HEAD_DIM
MOBA_BLOCK = 256
MOBA_TOPK = 3
MOBA_Q_CHUNK = 32
ALIBI_MAX = 8.0
RMS_EPS = 1e-6
SCALE = HEAD_DIM ** -0.5

kernel_name = "hybrid_fox_dilated_moba_decode_step"


def rmsnorm(x, g):
    xf = x.astype(jnp.float32)
    y = xf * lax.rsqrt(jnp.mean(xf * xf, axis=-1, keepdims=True) + RMS_EPS)
    return (y * g.astype(jnp.float32)).astype(x.dtype)


def heads(t, n):
    return t.reshape(t.shape[:-1] + (n, HEAD_DIM))


def alibi_slopes(n):
    return jnp.exp2(-ALIBI_MAX * jnp.arange(1, n + 1, dtype=jnp.float32) / n)


def gated_out(x, o, gate, w_out):
    o = o.reshape(o.shape[:2] + (-1,))
    return x + (o * jax.nn.silu(gate)) @ w_out


def gather_pages(pool, page_table):
    g = pool[page_table]
    return g.reshape((g.shape[0], g.shape[1] * g.shape[2]) + g.shape[3:])


def fox_attend(q, k, v, cq, ck, qpos, kpos):
    s = jnp.einsum('bqhd,bkhd->bhqk', q, k).astype(jnp.float32) * SCALE
    s = s + jnp.swapaxes(cq, 1, 2)[..., :, None] - jnp.swapaxes(ck, 1, 2)[..., None, :]
    s = jnp.where(kpos[None, :] <= qpos[:, None], s, -jnp.inf)
    p = jax.nn.softmax(s, axis=-1).astype(v.dtype)
    return jnp.einsum('bhqk,bkhd->bqhd', p, v)


def fox_prompt(q, k, v, logf):
    B, S = q.shape[:2]
    c = jnp.cumsum(logf, axis=1)
    kpos = jnp.arange(S)

    def block(i):
        st = i * Q_BLOCK
        qb = lax.dynamic_slice_in_dim(q, st, Q_BLOCK, axis=1)
        cb = lax.dynamic_slice_in_dim(c, st, Q_BLOCK, axis=1)
        return fox_attend(qb, k, v, cb, c, st + jnp.arange(Q_BLOCK), kpos)

    o = lax.map(block, jnp.arange(S // Q_BLOCK))
    return jnp.moveaxis(o, 0, 1).reshape(B, S, FOX_HEADS, HEAD_DIM)


def fox_sample(q, k, v, logf, kv_past, lf_past):
    P, T = kv_past.shape[1], q.shape[1]
    k_all = jnp.concatenate([kv_past[:, :, 0], k], axis=1)
    v_all = jnp.concatenate([kv_past[:, :, 1], v], axis=1)
    c_all = jnp.cumsum(jnp.concatenate([lf_past.astype(jnp.float32), logf], axis=1), axis=1)
    return fox_attend(q, k_all, v_all, c_all[:, P:], c_all, P + jnp.arange(T), jnp.arange(P + T))


def dilated_attend(q, k, v, qidx, window, dilation, slopes):
    n_keys = window // dilation + 1
    dist = jnp.arange(n_keys, dtype=jnp.int32) * dilation
    kidx = qidx[:, None] - dist[None, :]
    valid = kidx >= 0
    kidx = jnp.maximum(kidx, 0)
    kg = jnp.take(k, kidx, axis=1)
    vg = jnp.take(v, kidx, axis=1)
    s = jnp.einsum('bqhd,bqnhd->bhqn', q, kg).astype(jnp.float32) * SCALE
    s = s - slopes[:, None, None] * dist.astype(jnp.float32)
    s = jnp.where(valid, s, -jnp.inf)
    m = jnp.max(s, axis=-1, keepdims=True)
    e = jnp.exp(s - m)
    den = jnp.sum(e, axis=-1, keepdims=True)
    o = jnp.einsum('bhqn,bqnhd->bqhd', (e / den).astype(v.dtype), vg)
    lse = jnp.swapaxes((m + jnp.log(den))[..., 0], 1, 2)
    return o, lse


def combine_groups(outs, lses):
    w = jax.nn.softmax(jnp.stack(lses, axis=0), axis=0)
    return jnp.einsum('gbth,gbthd->bthd', w.astype(outs[0].dtype), jnp.stack(outs, axis=0))


def dil_prompt(qs, ks, vs, slopes):
    B, S = qs[0].shape[:2]

    def block(i):
        st = i * Q_BLOCK
        qidx = st + jnp.arange(Q_BLOCK)
        outs, lses = [], []
        for g, (w, d) in enumerate(DIL_PATTERNS):
            qb = lax.dynamic_slice_in_dim(qs[g], st, Q_BLOCK, axis=1)
            o, l = dilated_attend(qb, ks[g], vs[g], qidx, w, d, slopes[g])
            outs.append(o)
            lses.append(l)
        return combine_groups(outs, lses)

    o = lax.map(block, jnp.arange(S // Q_BLOCK))
    return jnp.moveaxis(o, 0, 1).reshape(B, S, DIL_GROUP_HEADS, HEAD_DIM)


def moba_blocks(k, v):
    B, L, H, Dh = k.shape
    nb = -(-L // MOBA_BLOCK)
    pad = ((0, 0), (0, nb * MOBA_BLOCK - L), (0, 0), (0, 0))

    def blk(t):
        t = jnp.pad(t, pad).reshape(B, nb, MOBA_BLOCK, H, Dh)
        return jnp.transpose(t, (0, 3, 1, 2, 4))

    kb, vb = blk(k), blk(v)
    kmean = jnp.mean(kb.astype(jnp.float32), axis=3).astype(k.dtype)
    return kb, vb, kmean


def moba_attend(q, qpos, kb, vb, kmean, slopes):
    b, qc, H, _ = q.shape
    nb = kb.shape[2]
    topk = min(MOBA_TOPK, nb)
    f32 = jnp.float32
    own = qpos // MOBA_BLOCK
    gate = jnp.einsum('bqhd,bhnd->bqhn', q, kmean).astype(f32)
    past = jnp.arange(nb)[None, :] < own[:, None]
    gate = jnp.where(past[None, :, None, :], gate, -jnp.inf)
    gval, sel = lax.top_k(gate, topk)
    sel_ok = jnp.isfinite(gval)
    bi = jnp.arange(b)[:, None, None, None]
    hi = jnp.arange(H)[None, None, :, None]
    ksel = kb[bi, hi, sel]
    vsel = vb[bi, hi, sel]
    kown = kb[:, :, own]
    vown = vb[:, :, own]
    offs = jnp.arange(MOBA_BLOCK)
    sel_dist = qpos[None, :, None, None, None] - (sel[..., None] * MOBA_BLOCK + offs)
    own_dist = qpos[:, None] - (own[:, None] * MOBA_BLOCK + offs[None, :])
    s_sel = jnp.einsum('bqhd,bqhnkd->bqhnk', q, ksel).astype(f32) * SCALE \
        - slopes[None, None, :, None, None] * sel_dist.astype(f32)
    s_sel = jnp.where(sel_ok[..., None], s_sel, -jnp.inf).reshape(b, qc, H, topk * MOBA_BLOCK)
    s_own = jnp.einsum('bqhd,bhqkd->bqhk', q, kown).astype(f32) * SCALE \
        - slopes[None, None, :, None] * own_dist[None, :, None, :].astype(f32)
    s_own = jnp.where((own_dist >= 0)[None, :, None, :], s_own, -jnp.inf)
    p = jax.nn.softmax(jnp.concatenate([s_sel, s_own], axis=-1), axis=-1).astype(vb.dtype)
    p_sel = p[..., :topk * MOBA_BLOCK].reshape(b, qc, H, topk, MOBA_BLOCK)
    p_own = p[..., topk * MOBA_BLOCK:]
    return jnp.einsum('bqhnk,bqhnkd->bqhd', p_sel, vsel) + jnp.einsum('bqhk,bhqkd->bqhd', p_own, vown)


def moba_prompt(q, k, v, slopes):
    B, S = q.shape[:2]
    kb, vb, km = moba_blocks(k, v)

    def chunk(i):
        st = i * MOBA_Q_CHUNK
        qc = lax.dynamic_slice_in_dim(q, st, MOBA_Q_CHUNK, axis=1)
        return moba_attend(qc, st + jnp.arange(MOBA_Q_CHUNK), kb, vb, km, slopes)

    o = lax.map(chunk, jnp.arange(S // MOBA_Q_CHUNK))
    return jnp.moveaxis(o, 0, 1).reshape(B, S, MOBA_HEADS, HEAD_DIM)


def moba_sample(q, k, v, kv_past, slopes):
    P, T = kv_past.shape[1], q.shape[1]
    kb, vb, km = moba_blocks(jnp.concatenate([kv_past[:, :, 0], k], axis=1),
                             jnp.concatenate([kv_past[:, :, 1], v], axis=1))
    qpos = P + jnp.arange(T)

    def one(args):
        qs, kbs, vbs, kms = args
        return moba_attend(qs[None], qpos, kbs[None], vbs[None], kms[None], slopes)[0]

    return lax.map(one, (q, kb, vb, km))


def setup_inputs(seed: int = 0) -> dict:
    key = jax.random.key(seed)
    ks = jax.random.split(key, 24)
    f32 = jnp.float32
    n_pages = PAST_LEN // PAGE_SIZE
    n_phys = (5 * DEC_BATCH * n_pages) // 4

    def nrm(k, shape, scale=1.0):
        return scale * jax.random.normal(k, shape, f32)

    x_prompt = nrm(ks[0], (BATCH, SEQ, D_MODEL))
    x_sample = nrm(ks[1], (DEC_BATCH, DEC_SEQ, D_MODEL))
    cache_fox_kv = nrm(ks[2], (N_FOX, n_phys, PAGE_SIZE, 2, FOX_HEADS, HEAD_DIM))
    cache_fox_logf = jax.nn.log_sigmoid(FOX_BIAS_INIT + nrm(ks[3], (N_FOX, n_phys, PAGE_SIZE, FOX_HEADS)))
    cache_dil_kv0 = nrm(ks[4], (N_DIL, DEC_BATCH, min(DIL_PATTERNS[0][0], PAST_LEN), 2, DIL_GROUP_HEADS, HEAD_DIM))
    cache_dil_kv1 = nrm(ks[5], (N_DIL, DEC_BATCH, min(DIL_PATTERNS[1][0], PAST_LEN), 2, DIL_GROUP_HEADS, HEAD_DIM))
    cache_dil_kv2 = nrm(ks[6], (N_DIL, DEC_BATCH, min(DIL_PATTERNS[2][0], PAST_LEN), 2, DIL_GROUP_HEADS, HEAD_DIM))
    cache_moba_kv = nrm(ks[7], (N_MOBA, n_phys, PAGE_SIZE, 2, MOBA_HEADS, HEAD_DIM))
    perm = jax.random.permutation(ks[8], n_phys)
    page_table = perm[: DEC_BATCH * n_pages].reshape(DEC_BATCH, n_pages).astype(jnp.int32)
    fox_norm = 1.0 + nrm(ks[9], (N_FOX, D_MODEL), 0.01)
    fox_w_in = nrm(ks[10], (N_FOX, D_MODEL, 4 * FOX_WIDTH + FOX_HEADS), D_MODEL ** -0.5)
    fox_b_f = FOX_BIAS_INIT + nrm(ks[11], (N_FOX, FOX_HEADS), 0.5)
    fox_w_out = nrm(ks[12], (N_FOX, FOX_WIDTH, D_MODEL), FOX_WIDTH ** -0.5)
    dil_norm = 1.0 + nrm(ks[13], (N_DIL, D_MODEL), 0.01)
    dil_w_in = nrm(ks[14], (N_DIL, D_MODEL, 3 * DIL_QKV + DIL_WIDTH), D_MODEL ** -0.5)
    dil_w_out = nrm(ks[15], (N_DIL, DIL_WIDTH, D_MODEL), DIL_WIDTH ** -0.5)
    moba_norm = 1.0 + nrm(ks[16], (N_MOBA, D_MODEL), 0.01)
    moba_w_in = nrm(ks[17], (N_MOBA, D_MODEL, 4 * MOBA_WIDTH), D_MODEL ** -0.5)
    moba_w_out = nrm(ks[18], (N_MOBA, MOBA_WIDTH, D_MODEL), MOBA_WIDTH ** -0.5)
    final_norm = 1.0 + nrm(ks[19], (D_MODEL,), 0.01)
    return {"x_prompt": x_prompt, "x_sample": x_sample,
            "cache_fox_kv": cache_fox_kv, "cache_fox_logf": cache_fox_logf,
            "cache_dil_kv0": cache_dil_kv0, "cache_dil_kv1": cache_dil_kv1, "cache_dil_kv2": cache_dil_kv2,
            "cache_moba_kv": cache_moba_kv, "page_table": page_table,
            "fox_norm": fox_norm, "fox_w_in": fox_w_in, "fox_b_f": fox_b_f, "fox_w_out": fox_w_out,
            "dil_norm": dil_norm, "dil_w_in": dil_w_in, "dil_w_out": dil_w_out,
            "moba_norm": moba_norm, "moba_w_in": moba_w_in, "moba_w_out": moba_w_out,
            "final_norm": final_norm}


def reference(x_prompt, x_sample, cache_fox_kv, cache_fox_logf, cache_dil_kv0, cache_dil_kv1,
              cache_dil_kv2, cache_moba_kv, page_table, fox_norm, fox_w_in, fox_b_f, fox_w_out,
              dil_norm, dil_w_in, dil_w_out, moba_norm, moba_w_in, moba_w_out, final_norm):
    dil_bufs = (cache_dil_kv0, cache_dil_kv1, cache_dil_kv2)
    dil_sl = alibi_slopes(DIL_HEADS)
    dil_slopes = [dil_sl[g * DIL_GROUP_HEADS:(g + 1) * DIL_GROUP_HEADS] for g in range(DIL_GROUPS)]
    moba_slopes = alibi_slopes(MOBA_HEADS)
    xp, xs = x_prompt, x_sample
    T = xs.shape[1]
    fox_kv_p, fox_kv_s, fox_lf_p, fox_lf_s = [], [], [], []
    dil_p = [[] for _ in range(DIL_GROUPS)]
    dil_s = [[] for _ in range(DIL_GROUPS)]
    moba_p, moba_s = [], []
    fox_cuts = [FOX_WIDTH, 2 * FOX_WIDTH, 3 * FOX_WIDTH, 4 * FOX_WIDTH]
    dil_cuts = [DIL_QKV, 2 * DIL_QKV, 3 * DIL_QKV]
    moba_cuts = [MOBA_WIDTH, 2 * MOBA_WIDTH, 3 * MOBA_WIDTH]
    for i in range(DEPTH):
        kind, j = i % N_MIXERS, i // N_MIXERS
        if kind == 0:
            outs = []
            for grp, x in enumerate((xp, xs)):
                q, k, v, gate, fl = jnp.split(rmsnorm(x, fox_norm[j]) @ fox_w_in[j], fox_cuts, axis=-1)
                q, k, v = heads(q, FOX_HEADS), heads(k, FOX_HEADS), heads(v, FOX_HEADS)
                logf = jax.nn.log_sigmoid(fl.astype(jnp.float32) + fox_b_f[j].astype(jnp.float32))
                if grp == 0:
                    o = fox_prompt(q, k, v, logf)
                    fox_kv_p.append(jnp.stack([k, v], axis=2))
                    fox_lf_p.append(logf.astype(x.dtype))
                else:
                    kv_past = gather_pages(cache_fox_kv[j], page_table)
                    lf_past = gather_pages(cache_fox_logf[j], page_table)
                    o = fox_sample(q, k, v, logf, kv_past, lf_past)
                    fox_kv_s.append(jnp.stack([k, v], axis=2))
                    fox_lf_s.append(logf.astype(x.dtype))
                outs.append(gated_out(x, o, gate, fox_w_out[j]))
            xp, xs = outs
        elif kind == 1:
            outs = []
            for grp, x in enumerate((xp, xs)):
                q, k, v, gate = jnp.split(rmsnorm(x, dil_norm[j]) @ dil_w_in[j], dil_cuts, axis=-1)
                q, k, v = heads(q, DIL_HEADS), heads(k, DIL_HEADS), heads(v, DIL_HEADS)
                sl = lambda t, g: t[:, :, g * DIL_GROUP_HEADS:(g + 1) * DIL_GROUP_HEADS]
                qs = [sl(q, g) for g in range(DIL_GROUPS)]
                kg = [sl(k, g) for g in range(DIL_GROUPS)]
                vg = [sl(v, g) for g in range(DIL_GROUPS)]
                if grp == 0:
                    S = x.shape[1]
                    o = dil_prompt(qs, kg, vg, dil_slopes)
                    for g, (w, d) in enumerate(DIL_PATTERNS):
                        dil_p[g].append(jnp.stack([kg[g], vg[g]], axis=2)[:, S - min(w, S):])
                else:
                    o_l, l_l = [], []
                    for g, (w, d) in enumerate(DIL_PATTERNS):
                        buf = dil_bufs[g][j]
                        L = buf.shape[1]
                        k_all = jnp.concatenate([buf[:, :, 0], kg[g]], axis=1)
                        v_all = jnp.concatenate([buf[:, :, 1], vg[g]], axis=1)
                        og, lg = dilated_attend(qs[g], k_all, v_all, L + jnp.arange(T), w, d, dil_slopes[g])
                        o_l.append(og)
                        l_l.append(lg)
                        new_buf = jnp.concatenate([buf, jnp.stack([kg[g], vg[g]], axis=2)], axis=1)
                        dil_s[g].append(new_buf[:, T:])
                    o = combine_groups(o_l, l_l)
                outs.append(gated_out(x, o, gate, dil_w_out[j]))
            xp, xs = outs
        else:
            outs = []
            for grp, x in enumerate((xp, xs)):
                q, k, v, gate = jnp.split(rmsnorm(x, moba_norm[j]) @ moba_w_in[j], moba_cuts, axis=-1)
                q, k, v = heads(q, MOBA_HEADS), heads(k, MOBA_HEADS), heads(v, MOBA_HEADS)
                if grp == 0:
                    o = moba_prompt(q, k, v, moba_slopes)
                    moba_p.append(jnp.stack([k, v], axis=2))
                else:
                    kv_past = gather_pages(cache_moba_kv[j], page_table)
                    o = moba_sample(q, k, v, kv_past, moba_slopes)
                    moba_s.append(jnp.stack([k, v], axis=2))
                outs.append(gated_out(x, o, gate, moba_w_out[j]))
            xp, xs = outs
    y_prompt = rmsnorm(xp, final_norm)
    y_sample = rmsnorm(xs, final_norm)
    return (y_prompt, y_sample,
            jnp.stack(fox_kv_p), jnp.stack(fox_kv_s), jnp.stack(fox_lf_p), jnp.stack(fox_lf_s),
            jnp.stack(dil_p[0]), jnp.stack(dil_s[0]), jnp.stack(dil_p[1]), jnp.stack(dil_s[1]),
            jnp.stack(dil_p[2]), jnp.stack(dil_s[2]),
            jnp.stack(moba_p), jnp.stack(moba_s))
```

```python
import functools

import jax
import jax.numpy as jnp
from jax import lax
from jax.experimental import pallas as pl
from jax.experimental.pallas import tpu as pltpu

F32 = jnp.float32
BF16 = jnp.bfloat16

HEAD_DIM = 64
LANES = 128
RMS_EPS = 1e-6
SCALE = HEAD_DIM ** -0.5
ALIBI_MAX = 8.0
MASKED = -1e30
VMEM_LIMIT = 56 * 1024 * 1024

DIL_PATTERNS = ((128, 1), (512, 4), (2048, 16))
MOBA_BLOCK = 256
MOBA_TOPK = 3

_NT = (((1,), (1,)), ((), ()))


def _cparams(sem):
    return pltpu.CompilerParams(dimension_semantics=sem, vmem_limit_bytes=VMEM_LIMIT)


def _alibi_slopes(n):
    return jnp.exp2(-ALIBI_MAX * jnp.arange(1, n + 1, dtype=F32) / n)


def _rms_bf16(x, g):
    ms = jnp.mean(x * x, axis=-1, keepdims=True)
    return ((x * lax.rsqrt(ms + RMS_EPS)) * g).astype(BF16)


def _split3(x):
    hi = x.astype(BF16)
    r1 = x - hi.astype(F32)
    mid = r1.astype(BF16)
    lo = (r1 - mid.astype(F32)).astype(BF16)
    return hi, mid, lo


def _dot3(x, m01):
    hi, mid, lo = _split3(x)
    d = lambda a: jnp.dot(a, m01, preferred_element_type=F32)
    return d(hi) + d(mid) + d(lo)


def _head_mask(lane, e):
    return (lane >= HEAD_DIM * e) & (lane < HEAD_DIM * (e + 1))


PROJ_CHUNK = 512


def _proj_kernel(x_ref, g_ref, w_ref, *rest, wq, wg, forget):
    if forget:
        wf_ref, bf_ref, qkv_ref, kv_ref, gate_ref, lf_ref = rest
    else:
        qkv_ref, kv_ref, gate_ref = rest
    y = _rms_bf16(x_ref[...], g_ref[...])
    for c0 in range(0, 3 * wq + wg, PROJ_CHUNK):
        t = jnp.dot(y, w_ref[:, c0:c0 + PROJ_CHUNK], preferred_element_type=F32)
        if c0 < wq:
            qkv_ref[:, c0:c0 + PROJ_CHUNK] = (t * SCALE).astype(BF16)
        elif c0 < 3 * wq:
            kv_ref[:, c0 - wq:c0 - wq + PROJ_CHUNK] = t
            qkv_ref[:, c0:c0 + PROJ_CHUNK] = t.astype(BF16)
        else:
            gate_ref[:, c0 - 3 * wq:c0 - 3 * wq + PROJ_CHUNK] = t
    if forget:
        fl = jnp.dot(y, wf_ref[...], preferred_element_type=F32)
        lf_ref[...] = jax.nn.log_sigmoid(fl + bf_ref[...])


def _proj(x2, g, w_in, wq, wg, b_f=None):
    n, d = x2.shape
    wmain = 3 * wq + wg
    tm = min(512, n)
    wb = w_in.astype(BF16)
    row = lambda i: (i, 0)
    fixed = lambda i: (0, 0)
    in_specs = [pl.BlockSpec((tm, d), row), pl.BlockSpec((1, d), fixed), pl.BlockSpec((d, wmain), fixed)]
    args = [x2, g.reshape(1, d), wb[:, :wmain]]
    out_specs = [pl.BlockSpec((tm, 3 * wq), row), pl.BlockSpec((tm, 2 * wq), row), pl.BlockSpec((tm, wg), row)]
    out_shape = [jax.ShapeDtypeStruct((n, 3 * wq), BF16), jax.ShapeDtypeStruct((n, 2 * wq), F32),
                 jax.ShapeDtypeStruct((n, wg), F32)]
    if b_f is not None:
        nh = b_f.shape[0]
        in_specs += [pl.BlockSpec((d, nh), fixed), pl.BlockSpec((1, nh), fixed)]
        args += [wb[:, wmain:], b_f.reshape(1, nh)]
        out_specs.append(pl.BlockSpec((tm, nh), row))
        out_shape.append(jax.ShapeDtypeStruct((n, nh), F32))
    return pl.pallas_call(
        functools.partial(_proj_kernel, wq=wq, wg=wg, forget=b_f is not None),
        grid=(n // tm,),
        in_specs=in_specs, out_specs=out_specs, out_shape=out_shape,
        compiler_params=_cparams(("arbitrary",)),
        name="norm_proj",
    )(*args)


def _out_kernel(x_ref, gate_ref, w_ref, *rest, n_o, final):
    o_refs = rest[:n_o]
    rest = rest[n_o:]
    if n_o == 1:
        o = o_refs[0][...]
    else:
        lse_refs, rest = rest[:n_o], rest[n_o:]
        lses = [r[...] for r in lse_refs]
        m = functools.reduce(jnp.maximum, lses)
        es = [jnp.exp(l - m) for l in lses]
        den = functools.reduce(lambda a, b: a + b, es)
        o = functools.reduce(lambda a, b: a + b, [(e / den) * r[...] for e, r in zip(es, o_refs)])
    h = (o * jax.nn.silu(gate_ref[...])).astype(BF16)
    y = x_ref[...] + jnp.dot(h, w_ref[...], preferred_element_type=F32)
    if final:
        fg_ref, y_ref = rest
        ms = jnp.mean(y * y, axis=-1, keepdims=True)
        y_ref[...] = (y * lax.rsqrt(ms + RMS_EPS)) * fg_ref[...]
    else:
        rest[0][...] = y


def _gated_out(x2, os_, gate2, w_out, lses=(), final_g=None):
    n, d = x2.shape
    w = gate2.shape[1]
    tm = min(512, n)
    row = lambda i: (i, 0)
    fixed = lambda i: (0, 0)
    in_specs = [pl.BlockSpec((tm, d), row), pl.BlockSpec((tm, w), row), pl.BlockSpec((w, d), fixed)]
    in_specs += [pl.BlockSpec((tm, w), row)] * (len(os_) + len(lses))
    args = [x2, gate2, w_out.astype(BF16), *os_, *lses]
    if final_g is not None:
        in_specs.append(pl.BlockSpec((1, d), fixed))
        args.append(final_g.reshape(1, d))
    return pl.pallas_call(
        functools.partial(_out_kernel, n_o=len(os_), final=final_g is not None),
        grid=(n // tm,),
        in_specs=in_specs,
        out_specs=pl.BlockSpec((tm, d), row),
        out_shape=jax.ShapeDtypeStruct((n, d), F32),
        compiler_params=_cparams(("arbitrary",)),
        name="gated_out",
    )(*args)


def _cumsum_kernel(x_ref, o_ref, carry, *, cb):
    @pl.when(pl.program_id(1) == 0)
    def _():
        carry[...] = jnp.zeros_like(carry)

    r = lax.broadcasted_iota(jnp.int32, (cb, cb), 0)
    c = lax.broadcasted_iota(jnp.int32, (cb, cb), 1)
    upper = (r <= c).astype(BF16)
    cs = _dot3(x_ref[0], upper) + carry[...]
    o_ref[0] = cs
    carry[...] = cs[:, cb - 1:cb]


def _cumsum_lanes(xt):
    b, h, s = xt.shape
    cb = min(512, s)
    return pl.pallas_call(
        functools.partial(_cumsum_kernel, cb=cb),
        grid=(b, s // cb),
        in_specs=[pl.BlockSpec((1, h, cb), lambda i, j: (i, 0, j))],
        out_specs=pl.BlockSpec((1, h, cb), lambda i, j: (i, 0, j)),
        out_shape=jax.ShapeDtypeStruct((b, h, s), F32),
        scratch_shapes=[pltpu.VMEM((h, 1), F32)],
        compiler_params=_cparams(("arbitrary", "arbitrary")),
        name="fox_cumsum",
    )(xt)


def _flash_kernel(*refs, t, mode):
    if mode == "fox":
        q_ref, k_ref, v_ref, c_ref, o_ref, m_sc, l_sc, acc_sc = refs
    else:
        slope_ref, q_ref, k_ref, v_ref, sel_ref, o_ref, m_sc, l_sc, acc_sc = refs
    hp = pl.program_id(1)
    qi = pl.program_id(2)
    q = q_ref[0]
    lane = lax.broadcasted_iota(jnp.int32, (t, LANES), 1)
    row = lax.broadcasted_iota(jnp.int32, (t, t), 0)
    col = lax.broadcasted_iota(jnp.int32, (t, t), 1)
    if mode == "moba":
        dist0 = (row - col).astype(F32)
        sel = sel_ref[0]
    outs = []
    for e in range(2):
        qm = jnp.where(_head_mask(lane, e), q, jnp.zeros_like(q))
        m_sc[...] = jnp.full_like(m_sc, MASKED)
        l_sc[...] = jnp.zeros_like(l_sc)
        acc_sc[...] = jnp.zeros_like(acc_sc)
        if mode == "fox":
            c_q0 = c_ref[0, 0, qi][e:e + 1, 0:1]
        else:
            slope = slope_ref[2 * hp + e]

        def tile(j, diagonal):
            ks = pl.multiple_of(j * t, t)
            k = k_ref[0, pl.ds(ks, t), :]
            v = v_ref[0, pl.ds(ks, t), :]
            s = lax.dot_general(qm, k, _NT, preferred_element_type=F32)
            if mode == "fox":
                s = s + (c_q0 - c_ref[0, 0, j][e:e + 1, :])
            else:
                s = s - slope * (dist0 + ((qi - j) * t).astype(F32))
                if not diagonal:
                    s = s + jnp.sum(jnp.where(lane == e * (LANES // 2) + j, sel, 0.0), axis=1, keepdims=True)
            if diagonal:
                s = jnp.where(col <= row, s, MASKED)
            m_prev = m_sc[...]
            m_new = jnp.maximum(m_prev, jnp.max(s, axis=1, keepdims=True))
            alpha = jnp.exp(m_prev - m_new)
            p = jnp.exp(s - m_new)
            l_sc[...] = alpha * l_sc[...] + jnp.sum(p, axis=1, keepdims=True)
            acc_sc[...] = alpha * acc_sc[...] + jnp.dot(p.astype(BF16), v, preferred_element_type=F32)
            m_sc[...] = m_new

        def body(j, carry):
            tile(j, False)
            return carry

        lax.fori_loop(0, qi, body, 0)
        tile(qi, True)
        outs.append(acc_sc[...] / l_sc[...])
    o_ref[0] = jnp.where(lane < HEAD_DIM, outs[0], outs[1])


def _flash_prompt(qkv, extra, t, mode, slopes=None):
    b, s, w3 = qkv.shape
    w = w3 // 3
    hp = w // LANES
    nt = s // t
    in_specs = [pl.BlockSpec((1, t, LANES), lambda i, h, j: (i, j, h)),
                pl.BlockSpec((1, s, LANES), lambda i, h, j: (i, 0, hp + h)),
                pl.BlockSpec((1, s, LANES), lambda i, h, j: (i, 0, 2 * hp + h))]
    args = [qkv, qkv, qkv, extra]
    if mode == "fox":
        in_specs.append(pl.BlockSpec((1, 1, nt, 2, t), lambda i, h, j: (i, h, 0, 0, 0)))
    else:
        in_specs = [pl.BlockSpec(memory_space=pltpu.SMEM)] + in_specs
        in_specs.append(pl.BlockSpec((1, t, LANES), lambda i, h, j: (i, j, h)))
        args = [slopes] + args
    return pl.pallas_call(
        functools.partial(_flash_kernel, t=t, mode=mode),
        grid=(b, hp, nt),
        in_specs=in_specs,
        out_specs=pl.BlockSpec((1, t, LANES), lambda i, h, j: (i, j, h)),
        out_shape=jax.ShapeDtypeStruct((b, s, w), F32),
        scratch_shapes=[pltpu.VMEM((t, 1), F32), pltpu.VMEM((t, 1), F32), pltpu.VMEM((t, LANES), F32)],
        compiler_params=_cparams(("arbitrary", "arbitrary", "arbitrary")),
        name=mode + "_attn_prompt",
    )(*args)


def _kmean_kernel(k_ref, o_ref, *, nblk):
    k = k_ref[0]
    o_ref[0] = jnp.mean(k.reshape(nblk, MOBA_BLOCK, k.shape[-1]), axis=1)


def _moba_kmean(kv):
    b, s, w2 = kv.shape
    w = w2 // 2
    rows = min(2048, s)
    nblk = rows // MOBA_BLOCK
    return pl.pallas_call(
        functools.partial(_kmean_kernel, nblk=nblk),
        grid=(b, s // rows),
        in_specs=[pl.BlockSpec((1, rows, w), lambda i, j: (i, j, 0))],
        out_specs=pl.BlockSpec((1, nblk, w), lambda i, j: (i, j, 0)),
        out_shape=jax.ShapeDtypeStruct((b, s // MOBA_BLOCK, w), F32),
        compiler_params=_cparams(("arbitrary", "arbitrary")),
        name="moba_kmean",
    )(kv)


def _topk_rank_mask(g, n_idx, own):
    nb = g.shape[0]
    cnt = jnp.zeros(g.shape, jnp.int32)
    for m in range(nb):
        gm = g[m:m + 1, :]
        beats = (gm > g) | ((gm == g) & (m < n_idx))
        cnt = cnt + jnp.where(beats & (m < own), 1, 0)
    return (n_idx < own) & (cnt < MOBA_TOPK) & (jnp.abs(g) < jnp.inf)


def _moba_select_kernel(q_ref, km_ref, o_ref, *, t, nb):
    qi = pl.program_id(2)
    q = q_ref[0]
    km = km_ref[0].astype(BF16)
    klane = lax.broadcasted_iota(jnp.int32, (nb, LANES), 1)
    n_idx = lax.broadcasted_iota(jnp.int32, (nb, t), 0)
    own = (qi * t + lax.broadcasted_iota(jnp.int32, (1, t), 1)) // MOBA_BLOCK
    rows = []
    for e in range(2):
        kme = jnp.where(_head_mask(klane, e), km, jnp.zeros_like(km))
        g = lax.dot_general(kme, q, _NT, preferred_element_type=F32)
        rows.append(jnp.where(_topk_rank_mask(g, n_idx, own), 0.0, MASKED))
        if nb < LANES // 2:
            rows.append(jnp.full((LANES // 2 - nb, t), MASKED, F32))
    o_ref[0] = jnp.transpose(jnp.concatenate(rows, axis=0))


def _moba_select(qkv, kmean, t):
    b, s, w3 = qkv.shape
    w = w3 // 3
    hp = w // LANES
    nb = kmean.shape[1]
    return pl.pallas_call(
        functools.partial(_moba_select_kernel, t=t, nb=nb),
        grid=(b, hp, s // t),
        in_specs=[pl.BlockSpec((1, t, LANES), lambda i, h, j: (i, j, h)),
                  pl.BlockSpec((1, nb, LANES), lambda i, h, j: (i, 0, h))],
        out_specs=pl.BlockSpec((1, t, LANES), lambda i, h, j: (i, j, h)),
        out_shape=jax.ShapeDtypeStruct((b, s, w), F32),
        compiler_params=_cparams(("arbitrary", "arbitrary", "arbitrary")),
        name="moba_select",
    )(qkv, kmean)


def _dil_attn_kernel(slope_ref, q_ref, kp_ref, kc_ref, vp_ref, vc_ref, o_ref, lse_ref, *, t, d, g, gh):
    ut = pl.program_id(2)
    lane = lax.broadcasted_iota(jnp.int32, (t, LANES), 1)
    row = lax.broadcasted_iota(jnp.int32, (t, t), 0)
    col = lax.broadcasted_iota(jnp.int32, (t, t), 1)
    jc = row - col
    jp = jc + t
    ok_c = jc >= 0
    ok_p = (jp <= t) & (ut > 0)
    dist_c = (jc * d).astype(F32)
    dist_p = (jp * d).astype(F32)
    for pair in range(gh // 2):
        sl = slice(pair * LANES, (pair + 1) * LANES)
        q, kp, kc, vp, vc = q_ref[0, :, sl], kp_ref[0, :, sl], kc_ref[0, :, sl], vp_ref[0, :, sl], vc_ref[0, :, sl]
        outs, lses = [], []
        for e in range(2):
            slope = slope_ref[g * gh + 2 * pair + e]
            qm = jnp.where(_head_mask(lane, e), q, jnp.zeros_like(q))
            sp = lax.dot_general(qm, kp, _NT, preferred_element_type=F32)
            sc = lax.dot_general(qm, kc, _NT, preferred_element_type=F32)
            sp = jnp.where(ok_p, sp - slope * dist_p, MASKED)
            sc = jnp.where(ok_c, sc - slope * dist_c, MASKED)
            m = jnp.maximum(jnp.max(sp, axis=1, keepdims=True), jnp.max(sc, axis=1, keepdims=True))
            ep = jnp.exp(sp - m)
            ec = jnp.exp(sc - m)
            den = jnp.sum(ep, axis=1, keepdims=True) + jnp.sum(ec, axis=1, keepdims=True)
            o = (jnp.dot((ep / den).astype(BF16), vp, preferred_element_type=F32)
                 + jnp.dot((ec / den).astype(BF16), vc, preferred_element_type=F32))
            outs.append(o)
            lses.append(m + jnp.log(den))
        o_ref[0, :, sl] = jnp.where(lane < HEAD_DIM, outs[0], outs[1])
        lse_ref[0, :, sl] = jnp.where(lane < HEAD_DIM, lses[0], lses[1])


def _dil_attn_prompt(qkv, slopes, g, window, d, gh):
    b, s, f3 = qkv.shape
    gw = gh * HEAD_DIM
    nblk = f3 // gw
    per = nblk // 3
    t = window // d
    su = s // d
    qv = qkv.reshape(b, su, d * f3)
    cur = lambda sec: (lambda i, r, u: (i, u, r * nblk + sec * per + g))
    prev = lambda sec: (lambda i, r, u: (i, jnp.maximum(u - 1, 0), r * nblk + sec * per + g))
    blk = (1, t, gw)
    o, lse = pl.pallas_call(
        functools.partial(_dil_attn_kernel, t=t, d=d, g=g, gh=gh),
        grid=(b, d, su // t),
        in_specs=[pl.BlockSpec(memory_space=pltpu.SMEM),
                  pl.BlockSpec(blk, cur(0)), pl.BlockSpec(blk, prev(1)), pl.BlockSpec(blk, cur(1)),
                  pl.BlockSpec(blk, prev(2)), pl.BlockSpec(blk, cur(2))],
        out_specs=[pl.BlockSpec(blk, lambda i, r, u: (i, u, r))] * 2,
        out_shape=[jax.ShapeDtypeStruct((b, su, d * gw), F32)] * 2,
        compiler_params=_cparams(("arbitrary", "arbitrary", "arbitrary")),
        name="dil_attn_prompt",
    )(slopes, qv, qv, qv, qv, qv)
    return o.reshape(b, s, gw), lse.reshape(b, s, gw)


def _fox_prompt_layer(x, g, w_in, b_f, w_out, final_g=None):
    b, s, d = x.shape
    x2 = x.reshape(b * s, d)
    nh = b_f.shape[0]
    w = nh * HEAD_DIM
    qkv, kv, gate, lf = _proj(x2, g, w_in, w, w, b_f)
    t = min(256, s)
    c = _cumsum_lanes(jnp.transpose(lf.reshape(b, s, nh), (0, 2, 1)))
    c_tiles = jnp.transpose(c.reshape(b, nh // 2, 2, s // t, t), (0, 1, 3, 2, 4))
    o = _flash_prompt(qkv.reshape(b, s, 3 * w), c_tiles, t, "fox")
    y = _gated_out(x2, [o.reshape(b * s, w)], gate, w_out, final_g=final_g)
    return y.reshape(b, s, d), kv.reshape(b, s, 2, nh, HEAD_DIM), lf.reshape(b, s, nh)


def _dil_prompt_layer(x, g, w_in, w_out):
    b, s, d = x.shape
    x2 = x.reshape(b * s, d)
    ng = len(DIL_PATTERNS)
    wg = w_out.shape[0]
    gh = wg // HEAD_DIM
    wq = ng * wg
    qkv, kv, gate = _proj(x2, g, w_in, wq, wg)
    slopes = _alibi_slopes(ng * gh)
    qkv3 = qkv.reshape(b, s, 3 * wq)
    os_, lses = [], []
    for gi, (window, dil) in enumerate(DIL_PATTERNS):
        o, lse = _dil_attn_prompt(qkv3, slopes, gi, window, dil, gh)
        os_.append(o.reshape(b * s, wg))
        lses.append(lse.reshape(b * s, wg))
    y = _gated_out(x2, os_, gate, w_out, lses=lses)
    kv3 = kv.reshape(b, s, 2, ng, gh, HEAD_DIM)
    caches = [kv3[:, s - min(window, s):, :, gi] for gi, (window, _) in enumerate(DIL_PATTERNS)]
    return y.reshape(b, s, d), caches


def _moba_prompt_layer(x, g, w_in, w_out):
    b, s, d = x.shape
    x2 = x.reshape(b * s, d)
    w = w_out.shape[0]
    nh = w // HEAD_DIM
    qkv, kv, gate = _proj(x2, g, w_in, w, w)
    qkv3 = qkv.reshape(b, s, 3 * w)
    kmean = _moba_kmean(kv.reshape(b, s, 2 * w))
    t = MOBA_BLOCK
    sel = _moba_select(qkv3, kmean, t)
    o = _flash_prompt(qkv3, sel, t, "moba", slopes=_alibi_slopes(nh))
    y = _gated_out(x2, [o.reshape(b * s, w)], gate, w_out)
    return y.reshape(b, s, d), kv.reshape(b, s, 2, nh, HEAD_DIM)


QROWS = 8


def _paged_kernel(pt_ref, *refs, mode, nh, npg, pg, nq):
    del pt_ref
    if mode == "fox":
        q_ref, kv_ref, lf_ref, kn_ref, vn_ref, lfn_ref, o_ref, m_sc, l_sc, acc_sc, carry = refs
    else:
        slope_ref, q_ref, kv_ref, kn_ref, vn_ref, o_ref, m_sc, l_sc, acc_sc, ksum = refs
    p = pl.program_id(1)
    past = npg * pg
    if mode == "fox":
        @pl.when(p == 0)
        def _():
            carry[...] = jnp.zeros_like(carry)

        r = lax.broadcasted_iota(jnp.int32, (pg, pg), 0)
        c = lax.broadcasted_iota(jnp.int32, (pg, pg), 1)
        cs = _dot3(lf_ref[...], (r <= c).astype(BF16)) + carry[...]
        carry[...] = cs[:, pg - 1:pg]
    else:
        ppb = MOBA_BLOCK // pg
        rowq = lax.broadcasted_iota(jnp.int32, (QROWS, pg), 0)
        colk = lax.broadcasted_iota(jnp.int32, (QROWS, pg), 1)
        dist = (past + rowq - (p * pg + colk)).astype(F32)

        @pl.when(p % ppb == 0)
        def _():
            for h in range(nh):
                ksum[h, p // ppb] = kv_ref[0, h]

        @pl.when(p % ppb != 0)
        def _():
            for h in range(nh):
                ksum[h, p // ppb] += kv_ref[0, h]

    for h in range(nh):
        k_t = kv_ref[0, h].astype(BF16)
        v_t = kv_ref[1, h].astype(BF16)
        s = jnp.dot(q_ref[0, h], k_t, preferred_element_type=F32)
        if mode == "fox":
            s = s - cs[h:h + 1, :]
        else:
            s = s - slope_ref[h] * dist
        m = jnp.max(s, axis=1, keepdims=True)
        e = jnp.exp(s - m)
        m_sc[h, p] = m
        l_sc[h, p] = jnp.sum(e, axis=1, keepdims=True)
        acc_sc[h, p] = lax.dot_general(e.astype(BF16), v_t, _NT, preferred_element_type=F32)

    @pl.when(p == npg - 1)
    def _():
        rq = lax.broadcasted_iota(jnp.int32, (QROWS, QROWS), 0)
        cn = lax.broadcasted_iota(jnp.int32, (QROWS, QROWS), 1)
        new_ok = (cn <= rq) & (cn < nq)
        if mode == "fox":
            up8 = (rq <= cn).astype(BF16)
            csn = _dot3(lfn_ref[0], up8) + carry[...]
        else:
            nblk = npg // ppb
            blk_lane = lax.broadcasted_iota(jnp.int32, (QROWS, nblk), 1)
            km_lane = lax.broadcasted_iota(jnp.int32, (q_ref.shape[3], nblk), 1)
        for h in range(nh):
            q = q_ref[0, h]
            sn = lax.dot_general(q, kn_ref[0, h], _NT, preferred_element_type=F32)
            if mode == "fox":
                sn = sn - csn[h:h + 1, :]
            else:
                sn = sn - slope_ref[h] * (rq - cn).astype(F32)
            sn = jnp.where(new_ok, sn, MASKED)
            mn = jnp.max(sn, axis=1, keepdims=True)
            en = jnp.exp(sn - mn)
            ln = jnp.sum(en, axis=1, keepdims=True)
            an = jnp.dot(en.astype(BF16), vn_ref[0, h], preferred_element_type=F32)
            mp = m_sc[h]
            if mode == "moba":
                kmean = jnp.zeros(km_lane.shape, F32)
                for n in range(nblk):
                    kmean = jnp.where(km_lane == n, jnp.sum(ksum[h, n], axis=1, keepdims=True), kmean)
                kmean = kmean * (1.0 / MOBA_BLOCK)
                g = jnp.dot(q, kmean.astype(BF16), preferred_element_type=F32)
                cnt = jnp.zeros(g.shape, jnp.int32)
                for m_ in range(nblk):
                    gm = g[:, m_:m_ + 1]
                    cnt = cnt + jnp.where((gm > g) | ((gm == g) & (m_ < blk_lane)), 1, 0)
                sel = (cnt < MOBA_TOPK) & (jnp.abs(g) < jnp.inf)
                selp = jnp.stack([sel[:, n // ppb:n // ppb + 1] for n in range(npg)], axis=0)
                mp = jnp.where(selp, mp, MASKED)
            mx = jnp.maximum(jnp.max(mp, axis=0), mn)
            wp = jnp.exp(mp - mx)
            wn = jnp.exp(mn - mx)
            num = jnp.sum(wp * acc_sc[h], axis=0) + wn * an
            den = jnp.sum(wp * l_sc[h], axis=0) + wn * ln
            o_ref[0, h] = num / den


def _pad_rows(x):
    x = jnp.transpose(x, (0, 2, 1, 3))
    return jnp.pad(x, ((0, 0), (0, 0), (0, QROWS - x.shape[2]), (0, 0)))


def _paged_attn(mode, layer, page_table, qkv, cache_kv, nh, cache_lf=None, lf_new=None, slopes=None):
    db, tq, w3 = qkv.shape
    w = w3 // 3
    nl, nphys, pg = cache_kv.shape[:3]
    npg = page_table.shape[1]
    hd = HEAD_DIM
    split = lambda i: _pad_rows(qkv[:, :, i * w:(i + 1) * w].reshape(db, tq, nh, hd))
    q8, k8, v8 = split(0), split(1), split(2)
    pages = jnp.transpose(cache_kv, (0, 1, 3, 4, 5, 2))
    hblk = pl.BlockSpec((1, nh, QROWS, hd), lambda b, p, pt: (b, 0, 0, 0))
    in_specs = [hblk, pl.BlockSpec((None, None, 2, nh, hd, pg), lambda b, p, pt: (layer, pt[b, p], 0, 0, 0, 0))]
    args = [q8, pages]
    scratch = [pltpu.VMEM((nh, npg, QROWS, 1), F32), pltpu.VMEM((nh, npg, QROWS, 1), F32),
               pltpu.VMEM((nh, npg, QROWS, hd), F32)]
    if mode == "fox":
        in_specs.append(pl.BlockSpec((None, None, nh, pg), lambda b, p, pt: (layer, pt[b, p], 0, 0)))
        args.append(jnp.transpose(cache_lf, (0, 1, 3, 2)))
    in_specs += [hblk, hblk]
    args += [k8, v8]
    if mode == "fox":
        lfn = jnp.pad(jnp.transpose(lf_new, (0, 2, 1)), ((0, 0), (0, 0), (0, QROWS - tq)))
        in_specs.append(pl.BlockSpec((1, nh, QROWS), lambda b, p, pt: (b, 0, 0)))
        args.append(lfn)
        scratch.append(pltpu.VMEM((nh, 1), F32))
    else:
        in_specs = [pl.BlockSpec(memory_space=pltpu.SMEM)] + in_specs
        args = [slopes] + args
        scratch.append(pltpu.VMEM((nh, npg * pg // MOBA_BLOCK, hd, pg), F32))
    o8 = pl.pallas_call(
        functools.partial(_paged_kernel, mode=mode, nh=nh, npg=npg, pg=pg, nq=tq),
        grid_spec=pltpu.PrefetchScalarGridSpec(
            num_scalar_prefetch=1, grid=(db, npg), in_specs=in_specs, out_specs=hblk, scratch_shapes=scratch),
        out_shape=jax.ShapeDtypeStruct((db, nh, QROWS, hd), F32),
        compiler_params=_cparams(("arbitrary", "arbitrary")),
        name=mode + "_attn_sample",
    )(page_table, *args)
    return jnp.transpose(o8[:, :, :tq], (0, 2, 1, 3)).reshape(db, tq, w)


def _dil_step_kernel(slope_ref, q_ref, kn_ref, vn_ref, nf_ref, b0_ref, b1_ref, b2_ref,
                     o_ref, n0_ref, n1_ref, n2_ref, *, gh, nq):
    h = pl.program_id(1)
    rq = lax.broadcasted_iota(jnp.int32, (QROWS, QROWS), 0)
    cn = lax.broadcasted_iota(jnp.int32, (QROWS, QROWS), 1)
    jj = lax.broadcasted_iota(jnp.int32, (QROWS, LANES), 0)
    ll = lax.broadcasted_iota(jnp.int32, (QROWS, LANES), 1)
    place = ((ll == LANES - nq + jj) & (jj < nq)).astype(BF16)
    tail = lax.broadcasted_iota(jnp.int32, (q_ref.shape[-1], LANES), 1) >= LANES - nq
    tn = (((0,), (0,)), ((), ()))
    outs, lses = [], []
    for gi, ((window, d), buf, nbuf) in enumerate(zip(DIL_PATTERNS, (b0_ref, b1_ref, b2_ref), (n0_ref, n1_ref, n2_ref))):
        length = buf.shape[-1]
        slope = slope_ref[gi * gh + h]
        q = q_ref[0, gi, 0]
        s = jnp.dot(q, buf[0, 0, 0].astype(BF16), preferred_element_type=F32)
        t_i = lax.broadcasted_iota(jnp.int32, (QROWS, length), 0)
        l_i = lax.broadcasted_iota(jnp.int32, (QROWS, length), 1)
        ok = (l_i >= t_i) & (((l_i - t_i) & (d - 1)) == 0)
        s = jnp.where(ok, s - slope * (length + t_i - l_i).astype(F32), MASKED)
        sn = lax.dot_general(q, kn_ref[0, gi, 0], _NT, preferred_element_type=F32)
        okn = (cn <= rq) & (((rq - cn) & (d - 1)) == 0) & (cn < nq)
        sn = jnp.where(okn, sn - slope * (rq - cn).astype(F32), MASKED)
        m = jnp.maximum(jnp.max(s, axis=1, keepdims=True), jnp.max(sn, axis=1, keepdims=True))
        e = jnp.exp(s - m)
        en = jnp.exp(sn - m)
        den = jnp.sum(e, axis=1, keepdims=True) + jnp.sum(en, axis=1, keepdims=True)
        o = lax.dot_general((e / den).astype(BF16), buf[0, 1, 0].astype(BF16), _NT, preferred_element_type=F32)
        outs.append(o + jnp.dot((en / den).astype(BF16), vn_ref[0, gi, 0], preferred_element_type=F32))
        lses.append(m + jnp.log(den))
        for kv in range(2):
            rolled = pltpu.roll(buf[0, kv, 0], length - nq, axis=1)
            new_t = sum(lax.dot_general(a, place, tn, preferred_element_type=F32)
                        for a in _split3(nf_ref[0, kv, gi, 0]))
            if length > LANES:
                nbuf[0, kv, 0, :, :length - LANES] = rolled[:, :length - LANES]
            nbuf[0, kv, 0, :, length - LANES:] = jnp.where(tail, new_t, rolled[:, length - LANES:])
    mx = functools.reduce(jnp.maximum, lses)
    es = [jnp.exp(l - mx) for l in lses]
    tot = functools.reduce(lambda a, b: a + b, es)
    o_ref[0, 0] = functools.reduce(lambda a, b: a + b, [(e / tot) * o for e, o in zip(es, outs)])


def _dil_step(qkv, kv_new, bufs, slopes, gh):
    db, tq, f3 = qkv.shape
    f = f3 // 3
    hd = HEAD_DIM
    ng = f // (gh * hd)
    split = lambda i: _pad_rows(qkv[:, :, i * f:(i + 1) * f].reshape(db, tq, ng * gh, hd)).reshape(db, ng, gh, QROWS, hd)
    q8, k8, v8 = split(0), split(1), split(2)
    nf = jnp.transpose(kv_new.reshape(db, tq, 2, ng, gh, hd), (0, 2, 3, 4, 1, 5))
    nf = jnp.pad(nf, ((0, 0),) * 4 + ((0, QROWS - tq), (0, 0)))
    hblk = pl.BlockSpec((1, ng, 1, QROWS, hd), lambda b, h: (b, 0, h, 0, 0))
    views = [jnp.transpose(buf, (0, 2, 3, 4, 1)) for buf in bufs]
    bspecs = [pl.BlockSpec((1, 2, 1, hd, v.shape[-1]), lambda b, h: (b, 0, h, 0, 0)) for v in views]
    res = pl.pallas_call(
        functools.partial(_dil_step_kernel, gh=gh, nq=tq),
        grid=(db, gh),
        in_specs=[pl.BlockSpec(memory_space=pltpu.SMEM), hblk, hblk, hblk,
                  pl.BlockSpec((1, 2, ng, 1, QROWS, hd), lambda b, h: (b, 0, 0, h, 0, 0))] + bspecs,
        out_specs=[pl.BlockSpec((1, 1, QROWS, hd), lambda b, h: (b, h, 0, 0))] + bspecs,
        out_shape=[jax.ShapeDtypeStruct((db, gh, QROWS, hd), F32)]
        + [jax.ShapeDtypeStruct(v.shape, v.dtype) for v in views],
        compiler_params=_cparams(("arbitrary", "arbitrary")),
        name="dil_step_sample",
    )(slopes, q8, k8, v8, nf, *views)
    o = jnp.transpose(res[0][:, :, :tq], (0, 2, 1, 3)).reshape(db, tq, gh * hd)
    return o, [jnp.transpose(r, (0, 4, 1, 2, 3)) for r in res[1:]]


def _fox_sample_layer(x, g, w_in, b_f, w_out, layer, cache_kv, cache_lf, page_table, final_g=None):
    db, tq, d = x.shape
    x2 = x.reshape(db * tq, d)
    nh = b_f.shape[0]
    w = nh * HEAD_DIM
    qkv, kv, gate, lf = _proj(x2, g, w_in, w, w, b_f)
    lf3 = lf.reshape(db, tq, nh)
    o = _paged_attn("fox", layer, page_table, qkv.reshape(db, tq, 3 * w), cache_kv, nh, cache_lf=cache_lf, lf_new=lf3)
    y = _gated_out(x2, [o.reshape(db * tq, w)], gate, w_out, final_g=final_g)
    return y.reshape(db, tq, d), kv.reshape(db, tq, 2, nh, HEAD_DIM), lf3


def _dil_sample_layer(x, g, w_in, w_out, bufs):
    db, tq, d = x.shape
    x2 = x.reshape(db * tq, d)
    ng = len(DIL_PATTERNS)
    wg = w_out.shape[0]
    gh = wg // HEAD_DIM
    wq = ng * wg
    qkv, kv, gate = _proj(x2, g, w_in, wq, wg)
    o, new_bufs = _dil_step(qkv.reshape(db, tq, 3 * wq), kv.reshape(db, tq, 2 * wq), bufs, _alibi_slopes(ng * gh), gh)
    y = _gated_out(x2, [o.reshape(db * tq, wg)], gate, w_out)
    return y.reshape(db, tq, d), new_bufs


def _moba_sample_layer(x, g, w_in, w_out, layer, cache_kv, page_table):
    db, tq, d = x.shape
    x2 = x.reshape(db * tq, d)
    w = w_out.shape[0]
    nh = w // HEAD_DIM
    qkv, kv, gate = _proj(x2, g, w_in, w, w)
    o = _paged_attn("moba", layer, page_table, qkv.reshape(db, tq, 3 * w), cache_kv, nh, slopes=_alibi_slopes(nh))
    y = _gated_out(x2, [o.reshape(db * tq, w)], gate, w_out)
    return y.reshape(db, tq, d), kv.reshape(db, tq, 2, nh, HEAD_DIM)


def kernel(x_prompt, x_sample, cache_fox_kv, cache_fox_logf, cache_dil_kv0, cache_dil_kv1, cache_dil_kv2, cache_moba_kv, page_table, fox_norm, fox_w_in, fox_b_f, fox_w_out, dil_norm, dil_w_in, dil_w_out, moba_norm, moba_w_in, moba_w_out, final_norm):
    depth = fox_norm.shape[0] + dil_norm.shape[0] + moba_norm.shape[0]
    dil_bufs = (cache_dil_kv0, cache_dil_kv1, cache_dil_kv2)
    xp, xs = x_prompt, x_sample
    fox_kv_p, fox_kv_s, fox_lf_p, fox_lf_s, moba_p, moba_s = [], [], [], [], [], []
    dil_p = [[] for _ in dil_bufs]
    dil_s = [[] for _ in dil_bufs]
    for i in range(depth):
        kind, j = i % 3, i // 3
        fg = final_norm if i == depth - 1 else None
        if kind == 0:
            xp, kv, lf = _fox_prompt_layer(xp, fox_norm[j], fox_w_in[j], fox_b_f[j], fox_w_out[j], fg)
            fox_kv_p.append(kv)
            fox_lf_p.append(lf)
            xs, kv, lf = _fox_sample_layer(xs, fox_norm[j], fox_w_in[j], fox_b_f[j], fox_w_out[j], j,
                                           cache_fox_kv, cache_fox_logf, page_table, fg)
            fox_kv_s.append(kv)
            fox_lf_s.append(lf)
        elif kind == 1:
            xp, caches = _dil_prompt_layer(xp, dil_norm[j], dil_w_in[j], dil_w_out[j])
            xs, bufs = _dil_sample_layer(xs, dil_norm[j], dil_w_in[j], dil_w_out[j], [b[j] for b in dil_bufs])
            for gi in range(len(dil_bufs)):
                dil_p[gi].append(caches[gi])
                dil_s[gi].append(bufs[gi])
        else:
            xp, kv = _moba_prompt_layer(xp, moba_norm[j], moba_w_in[j], moba_w_out[j])
            moba_p.append(kv)
            xs, kv = _moba_sample_layer(xs, moba_norm[j], moba_w_in[j], moba_w_out[j], j, cache_moba_kv, page_table)
            moba_s.append(kv)
    if depth % 3 != 1:
        raise NotImplementedError("final RMSNorm is fused into the last forgetting-attention layer")
    return (xp, xs,
            jnp.stack(fox_kv_p), jnp.stack(fox_kv_s), jnp.stack(fox_lf_p), jnp.stack(fox_lf_s),
            jnp.stack(dil_p[0]), jnp.stack(dil_s[0]), jnp.stack(dil_p[1]), jnp.stack(dil_s[1]),
            jnp.stack(dil_p[2]), jnp.stack(dil_s[2]),
            jnp.stack(moba_p), jnp.stack(moba_s))
```

```python
import functools

import jax
import jax.numpy as jnp
from jax import lax
from jax.experimental import pallas as pl
from jax.experimental.pallas import tpu as pltpu

F32 = jnp.float32
BF16 = jnp.bfloat16

HEAD_DIM = 64
LANES = 128
RMS_EPS = 1e-6
SCALE = HEAD_DIM ** -0.5
ALIBI_MAX = 8.0
MASKED = -1e30
VMEM_LIMIT = 56 * 1024 * 1024

DIL_PATTERNS = ((128, 1), (512, 4), (2048, 16))
MOBA_BLOCK = 256
MOBA_TOPK = 3
FLASH_TILE = 256

_NT = (((1,), (1,)), ((), ()))


def _cparams(sem):
    return pltpu.CompilerParams(dimension_semantics=sem, vmem_limit_bytes=VMEM_LIMIT)


def _alibi_slopes(n):
    return jnp.exp2(-ALIBI_MAX * jnp.arange(1, n + 1, dtype=F32) / n)


def _rms_bf16(x, g):
    ms = jnp.mean(x * x, axis=-1, keepdims=True)
    return ((x * lax.rsqrt(ms + RMS_EPS)) * g).astype(BF16)


def _split3(x):
    hi = x.astype(BF16)
    r1 = x - hi.astype(F32)
    mid = r1.astype(BF16)
    lo = (r1 - mid.astype(F32)).astype(BF16)
    return hi, mid, lo


def _dot3(x, m01):
    hi, mid, lo = _split3(x)
    d = lambda a: jnp.dot(a, m01, preferred_element_type=F32)
    return d(hi) + d(mid) + d(lo)


def _head_mask(lane, e):
    return (lane >= HEAD_DIM * e) & (lane < HEAD_DIM * (e + 1))


PROJ_CHUNK = 512


def _proj_kernel(x_ref, g_ref, w_ref, *rest, wq, wg, forget):
    if forget:
        wf_ref, bf_ref, qkv_ref, kv_ref, gate_ref, lf_ref = rest
    else:
        qkv_ref, kv_ref, gate_ref = rest
    y = _rms_bf16(x_ref[...], g_ref[...])
    for c0 in range(0, 3 * wq + wg, PROJ_CHUNK):
        t = jnp.dot(y, w_ref[:, c0:c0 + PROJ_CHUNK], preferred_element_type=F32)
        if c0 < wq:
            qkv_ref[:, c0:c0 + PROJ_CHUNK] = (t * SCALE).astype(BF16)
        elif c0 < 3 * wq:
            kv_ref[:, c0 - wq:c0 - wq + PROJ_CHUNK] = t
            qkv_ref[:, c0:c0 + PROJ_CHUNK] = t.astype(BF16)
        else:
            gate_ref[:, c0 - 3 * wq:c0 - 3 * wq + PROJ_CHUNK] = t
    if forget:
        fl = jnp.dot(y, wf_ref[...], preferred_element_type=F32)
        lf_ref[...] = jax.nn.log_sigmoid(fl + bf_ref[...])


def _proj(x2, g, w_in, wq, wg, b_f=None):
    n, d = x2.shape
    wmain = 3 * wq + wg
    tm = min(512, n)
    wb = w_in.astype(BF16)
    row = lambda i: (i, 0)
    fixed = lambda i: (0, 0)
    in_specs = [pl.BlockSpec((tm, d), row), pl.BlockSpec((1, d), fixed), pl.BlockSpec((d, wmain), fixed)]
    args = [x2, g.reshape(1, d), wb[:, :wmain]]
    out_specs = [pl.BlockSpec((tm, 3 * wq), row), pl.BlockSpec((tm, 2 * wq), row), pl.BlockSpec((tm, wg), row)]
    out_shape = [jax.ShapeDtypeStruct((n, 3 * wq), BF16), jax.ShapeDtypeStruct((n, 2 * wq), F32),
                 jax.ShapeDtypeStruct((n, wg), F32)]
    if b_f is not None:
        nh = b_f.shape[0]
        in_specs += [pl.BlockSpec((d, nh), fixed), pl.BlockSpec((1, nh), fixed)]
        args += [wb[:, wmain:], b_f.reshape(1, nh)]
        out_specs.append(pl.BlockSpec((tm, nh), row))
        out_shape.append(jax.ShapeDtypeStruct((n, nh), F32))
    return pl.pallas_call(
        functools.partial(_proj_kernel, wq=wq, wg=wg, forget=b_f is not None),
        grid=(n // tm,),
        in_specs=in_specs, out_specs=out_specs, out_shape=out_shape,
        compiler_params=_cparams(("arbitrary",)),
        name="norm_proj",
    )(*args)


def _out_kernel(x_ref, gate_ref, w_ref, *rest, n_o, final):
    o_refs = rest[:n_o]
    rest = rest[n_o:]
    if n_o == 1:
        o = o_refs[0][...]
    else:
        lse_refs, rest = rest[:n_o], rest[n_o:]
        lses = [r[...] for r in lse_refs]
        m = functools.reduce(jnp.maximum, lses)
        es = [jnp.exp(l - m) for l in lses]
        den = functools.reduce(lambda a, b: a + b, es)
        o = functools.reduce(lambda a, b: a + b, [(e / den) * r[...] for e, r in zip(es, o_refs)])
    h = (o * jax.nn.silu(gate_ref[...])).astype(BF16)
    y = x_ref[...] + jnp.dot(h, w_ref[...], preferred_element_type=F32)
    if final:
        fg_ref, y_ref = rest
        ms = jnp.mean(y * y, axis=-1, keepdims=True)
        y_ref[...] = (y * lax.rsqrt(ms + RMS_EPS)) * fg_ref[...]
    else:
        rest[0][...] = y


def _gated_out(x2, os_, gate2, w_out, lses=(), final_g=None):
    n, d = x2.shape
    w = gate2.shape[1]
    tm = min(512, n)
    row = lambda i: (i, 0)
    fixed = lambda i: (0, 0)
    in_specs = [pl.BlockSpec((tm, d), row), pl.BlockSpec((tm, w), row), pl.BlockSpec((w, d), fixed)]
    in_specs += [pl.BlockSpec((tm, w), row)] * (len(os_) + len(lses))
    args = [x2, gate2, w_out.astype(BF16), *os_, *lses]
    if final_g is not None:
        in_specs.append(pl.BlockSpec((1, d), fixed))
        args.append(final_g.reshape(1, d))
    return pl.pallas_call(
        functools.partial(_out_kernel, n_o=len(os_), final=final_g is not None),
        grid=(n // tm,),
        in_specs=in_specs,
        out_specs=pl.BlockSpec((tm, d), row),
        out_shape=jax.ShapeDtypeStruct((n, d), F32),
        compiler_params=_cparams(("arbitrary",)),
        name="gated_out",
    )(*args)


def _place3(parts, nh):
    rr = lax.broadcasted_iota(jnp.int32, (nh, LANES), 0)
    ll = lax.broadcasted_iota(jnp.int32, (nh, LANES), 1)
    return sum(jnp.dot(p, (ll == 3 * rr + k).astype(BF16), preferred_element_type=F32) for k, p in enumerate(parts))


def _fox_keybias_kernel(lf_ref, o_ref, carry, *, cb, nh):
    @pl.when(pl.program_id(1) == 0)
    def _():
        carry[...] = jnp.zeros_like(carry)

    r = lax.broadcasted_iota(jnp.int32, (cb, cb), 0)
    c = lax.broadcasted_iota(jnp.int32, (cb, cb), 1)
    lower = (c <= r).astype(BF16)
    cs = sum(jnp.dot(lower, p, preferred_element_type=F32) for p in _split3(lf_ref[0])) + carry[...]
    carry[...] = cs[cb - 1:cb, :]
    o_ref[0] = _place3(_split3(-cs), nh).astype(BF16)


def _fox_keybias(lf):
    b, s, nh = lf.shape
    cb = min(512, s)
    return pl.pallas_call(
        functools.partial(_fox_keybias_kernel, cb=cb, nh=nh),
        grid=(b, s // cb),
        in_specs=[pl.BlockSpec((1, cb, nh), lambda i, j: (i, j, 0))],
        out_specs=pl.BlockSpec((1, cb, LANES), lambda i, j: (i, j, 0)),
        out_shape=jax.ShapeDtypeStruct((b, s, LANES), BF16),
        scratch_shapes=[pltpu.VMEM((1, nh), F32)],
        compiler_params=_cparams(("arbitrary", "arbitrary")),
        name="fox_keybias",
    )(lf)


def _moba_keyrows_kernel(slope_ref, o_ref, *, cb, nh):
    base = pl.program_id(0) * cb
    pos = base + lax.broadcasted_iota(jnp.int32, (cb, nh), 0)
    placed = _place3(_split3(pos.astype(F32) * slope_ref[...]), nh)
    posl = base + lax.broadcasted_iota(jnp.int32, (cb, LANES), 0)
    lane = lax.broadcasted_iota(jnp.int32, (cb, LANES), 1)
    onehot = jnp.where(lane - LANES // 2 == posl // MOBA_BLOCK, 1.0, 0.0)
    o_ref[0] = (placed + onehot).astype(BF16)


def _moba_keyrows(s, slopes):
    nh = slopes.shape[0]
    cb = min(512, s)
    return pl.pallas_call(
        functools.partial(_moba_keyrows_kernel, cb=cb, nh=nh),
        grid=(s // cb,),
        in_specs=[pl.BlockSpec((1, nh), lambda j: (0, 0))],
        out_specs=pl.BlockSpec((1, cb, LANES), lambda j: (0, j, 0)),
        out_shape=jax.ShapeDtypeStruct((1, s, LANES), BF16),
        compiler_params=_cparams(("arbitrary",)),
        name="moba_keyrows",
    )(slopes.reshape(1, nh))


def _flash_kernel(*refs, t, nt, masked_blocks):
    if masked_blocks:
        q_ref, k_ref, kx_ref, v_ref, sel_ref, o_ref, vt_sc, rhs_sc = refs
    else:
        q_ref, k_ref, kx_ref, v_ref, o_ref, vt_sc, rhs_sc = refs
    hp = pl.program_id(1)
    qi = pl.program_id(2)
    half = LANES // 2

    @pl.when(qi == 0)
    def _():
        for j in range(nt):
            vt_sc[j] = jnp.transpose(v_ref[0, j * t:(j + 1) * t, :].astype(F32)).astype(BF16)

    q_t = jnp.transpose(q_ref[0].astype(F32))
    frow = lax.broadcasted_iota(jnp.int32, (LANES, t), 0)
    xrow = lax.broadcasted_iota(jnp.int32, (half, t), 0)
    for e in range(2):
        h = 2 * hp + e
        qm = jnp.where(_head_mask(frow, e), q_t, 0.0).astype(BF16)
        ones = jnp.where((xrow >= 3 * h) & (xrow < 3 * h + 3), 1.0, 0.0).astype(BF16)
        low = sel_ref[0, 0, e] if masked_blocks else jnp.zeros((half, t), BF16)
        rhs_sc[e] = jnp.concatenate([qm, ones, low], axis=0)
    krow = lax.broadcasted_iota(jnp.int32, (t, t), 0)
    qcol = lax.broadcasted_iota(jnp.int32, (t, t), 1)

    def scores(j):
        ks = pl.multiple_of(j * t, t)
        kc = jnp.concatenate([k_ref[0, pl.ds(ks, t), :], kx_ref[0, pl.ds(ks, t), :]], axis=1)
        return tuple(jnp.dot(kc, rhs_sc[e], preferred_element_type=F32) for e in range(2))

    def update(j, s2, state, diagonal):
        vt = vt_sc[j]
        new = []
        for e in range(2):
            m_prev, l_prev, acc = state[e]
            s = jnp.where(krow <= qcol, s2[e], MASKED) if diagonal else s2[e]
            m_new = jnp.maximum(m_prev, jnp.max(s, axis=0, keepdims=True))
            alpha = jnp.exp(m_prev - m_new)
            p = jnp.exp(s - m_new)
            l_new = alpha * l_prev + jnp.sum(p, axis=0, keepdims=True)
            pv = jnp.dot(vt[e * half:(e + 1) * half, :], p.astype(BF16), preferred_element_type=F32)
            new.append((m_new, l_new, alpha * acc + pv))
        return tuple(new)

    def body(j, carry):
        s2, state = carry
        s_next = scores(j + 1)
        return s_next, update(j, s2, state, False)

    init = tuple((jnp.full((1, t), MASKED, F32), jnp.zeros((1, t), F32), jnp.zeros((half, t), F32))
                 for _ in range(2))
    s2, state = lax.fori_loop(0, qi, body, (scores(0), init))
    state = update(qi, s2, state, True)
    o_t = jnp.concatenate([acc / l for _, l, acc in state], axis=0)
    o_ref[0] = jnp.transpose(o_t)


def _flash_prompt(qkv, keyrows, t, sel=None, name="attn_prompt"):
    b, s, w3 = qkv.shape
    w = w3 // 3
    hp = w // LANES
    nt = s // t
    per_batch = keyrows.shape[0] > 1
    in_specs = [pl.BlockSpec((1, t, LANES), lambda i, h, j: (i, j, h)),
                pl.BlockSpec((1, s, LANES), lambda i, h, j: (i, 0, hp + h)),
                pl.BlockSpec((1, s, LANES), lambda i, h, j: (i if per_batch else 0, 0, 0)),
                pl.BlockSpec((1, s, LANES), lambda i, h, j: (i, 0, 2 * hp + h))]
    args = [qkv, qkv, keyrows, qkv]
    if sel is not None:
        in_specs.append(pl.BlockSpec((1, 1, 2, LANES // 2, t), lambda i, h, j: (i, h, 0, 0, j)))
        args.append(sel)
    return pl.pallas_call(
        functools.partial(_flash_kernel, t=t, nt=nt, masked_blocks=sel is not None),
        grid=(b, hp, nt),
        in_specs=in_specs,
        out_specs=pl.BlockSpec((1, t, LANES), lambda i, h, j: (i, j, h)),
        out_shape=jax.ShapeDtypeStruct((b, s, w), F32),
        scratch_shapes=[pltpu.VMEM((nt, LANES, t), BF16), pltpu.VMEM((2, 2 * LANES, t), BF16)],
        compiler_params=_cparams(("arbitrary", "arbitrary", "arbitrary")),
        name=name,
    )(*args)


def _kmean_kernel(k_ref, o_ref, *, nblk):
    k = k_ref[0]
    o_ref[0] = jnp.mean(k.reshape(nblk, MOBA_BLOCK, k.shape[-1]), axis=1)


def _moba_kmean(kv):
    b, s, w2 = kv.shape
    w = w2 // 2
    rows = min(2048, s)
    nblk = rows // MOBA_BLOCK
    return pl.pallas_call(
        functools.partial(_kmean_kernel, nblk=nblk),
        grid=(b, s // rows),
        in_specs=[pl.BlockSpec((1, rows, w), lambda i, j: (i, j, 0))],
        out_specs=pl.BlockSpec((1, nblk, w), lambda i, j: (i, j, 0)),
        out_shape=jax.ShapeDtypeStruct((b, s // MOBA_BLOCK, w), F32),
        compiler_params=_cparams(("arbitrary", "arbitrary")),
        name="moba_kmean",
    )(kv)


def _topk_rank_mask(g, n_idx, own):
    nb = g.shape[0]
    cnt = jnp.zeros(g.shape, jnp.int32)
    for m in range(nb):
        gm = g[m:m + 1, :]
        beats = (gm > g) | ((gm == g) & (m < n_idx))
        cnt = cnt + jnp.where(beats & (m < own), 1, 0)
    return (n_idx < own) & (cnt < MOBA_TOPK) & (jnp.abs(g) < jnp.inf)


def _moba_select_kernel(q_ref, km_ref, o_ref, *, t, nb):
    qi = pl.program_id(2)
    q = q_ref[0]
    km = km_ref[0].astype(BF16)
    klane = lax.broadcasted_iota(jnp.int32, (nb, LANES), 1)
    n_idx = lax.broadcasted_iota(jnp.int32, (nb, t), 0)
    own = (qi * t + lax.broadcasted_iota(jnp.int32, (1, t), 1)) // MOBA_BLOCK
    for e in range(2):
        kme = jnp.where(_head_mask(klane, e), km, jnp.zeros_like(km))
        g = lax.dot_general(kme, q, _NT, preferred_element_type=F32)
        keep = _topk_rank_mask(g, n_idx, own) | (n_idx == own)
        bias = jnp.where(keep, 0.0, MASKED)
        if nb < LANES // 2:
            bias = jnp.concatenate([bias, jnp.zeros((LANES // 2 - nb, t), F32)], axis=0)
        o_ref[0, 0, e] = bias.astype(BF16)


def _moba_select(qkv, kmean, t):
    b, s, w3 = qkv.shape
    w = w3 // 3
    hp = w // LANES
    nb = kmean.shape[1]
    return pl.pallas_call(
        functools.partial(_moba_select_kernel, t=t, nb=nb),
        grid=(b, hp, s // t),
        in_specs=[pl.BlockSpec((1, t, LANES), lambda i, h, j: (i, j, h)),
                  pl.BlockSpec((1, nb, LANES), lambda i, h, j: (i, 0, h))],
        out_specs=pl.BlockSpec((1, 1, 2, LANES // 2, t), lambda i, h, j: (i, h, 0, 0, j)),
        out_shape=jax.ShapeDtypeStruct((b, hp, 2, LANES // 2, s), BF16),
        compiler_params=_cparams(("arbitrary", "arbitrary", "arbitrary")),
        name="moba_select",
    )(qkv, kmean)


def _dil_attn_kernel(slope_ref, q_ref, kp_ref, kc_ref, vp_ref, vc_ref, o_ref, lse_ref, *, t, d, g, gh):
    ut = pl.program_id(2)
    lane = lax.broadcasted_iota(jnp.int32, (t, LANES), 1)
    row = lax.broadcasted_iota(jnp.int32, (t, t), 0)
    col = lax.broadcasted_iota(jnp.int32, (t, t), 1)
    jc = row - col
    jp = jc + t
    ok_c = jc >= 0
    ok_p = (jp <= t) & (ut > 0)
    dist_c = (jc * d).astype(F32)
    dist_p = (jp * d).astype(F32)
    heads = [(pair, e) for pair in range(gh // 2) for e in range(2)]
    pair_lanes = lambda pair: slice(pair * LANES, (pair + 1) * LANES)
    raw = []
    for pair, e in heads:
        sl = pair_lanes(pair)
        q = q_ref[0, :, sl]
        qm = jnp.where(_head_mask(lane, e), q, jnp.zeros_like(q))
        raw.append((lax.dot_general(qm, kp_ref[0, :, sl], _NT, preferred_element_type=F32),
                    lax.dot_general(qm, kc_ref[0, :, sl], _NT, preferred_element_type=F32)))
    probs, lses = [], []
    for (pair, e), (sp, sc) in zip(heads, raw):
        slope = slope_ref[g * gh + 2 * pair + e]
        sp = jnp.where(ok_p, sp - slope * dist_p, MASKED)
        sc = jnp.where(ok_c, sc - slope * dist_c, MASKED)
        m = jnp.maximum(jnp.max(sp, axis=1, keepdims=True), jnp.max(sc, axis=1, keepdims=True))
        ep = jnp.exp(sp - m)
        ec = jnp.exp(sc - m)
        den = jnp.sum(ep, axis=1, keepdims=True) + jnp.sum(ec, axis=1, keepdims=True)
        probs.append(((ep / den).astype(BF16), (ec / den).astype(BF16)))
        lses.append(m + jnp.log(den))
    outs = [jnp.dot(pp, vp_ref[0, :, pair_lanes(pair)], preferred_element_type=F32)
            + jnp.dot(pc, vc_ref[0, :, pair_lanes(pair)], preferred_element_type=F32)
            for (pair, e), (pp, pc) in zip(heads, probs)]
    for pair in range(gh // 2):
        sl = pair_lanes(pair)
        o_ref[0, :, sl] = jnp.where(lane < HEAD_DIM, outs[2 * pair], outs[2 * pair + 1])
        lse_ref[0, :, sl] = jnp.where(lane < HEAD_DIM, lses[2 * pair], lses[2 * pair + 1])


def _dil_attn_prompt(qkv, slopes, g, window, d, gh):
    b, s, f3 = qkv.shape
    gw = gh * HEAD_DIM
    nblk = f3 // gw
    per = nblk // 3
    t = window // d
    su = s // d
    qv = qkv.reshape(b, su, d * f3)
    cur = lambda sec: (lambda i, r, u: (i, u, r * nblk + sec * per + g))
    prev = lambda sec: (lambda i, r, u: (i, jnp.maximum(u - 1, 0), r * nblk + sec * per + g))
    blk = (1, t, gw)
    o, lse = pl.pallas_call(
        functools.partial(_dil_attn_kernel, t=t, d=d, g=g, gh=gh),
        grid=(b, d, su // t),
        in_specs=[pl.BlockSpec(memory_space=pltpu.SMEM),
                  pl.BlockSpec(blk, cur(0)), pl.BlockSpec(blk, prev(1)), pl.BlockSpec(blk, cur(1)),
                  pl.BlockSpec(blk, prev(2)), pl.BlockSpec(blk, cur(2))],
        out_specs=[pl.BlockSpec(blk, lambda i, r, u: (i, u, r))] * 2,
        out_shape=[jax.ShapeDtypeStruct((b, su, d * gw), F32)] * 2,
        compiler_params=_cparams(("arbitrary", "arbitrary", "arbitrary")),
        name="dil_attn_prompt",
    )(slopes, qv, qv, qv, qv, qv)
    return o.reshape(b, s, gw), lse.reshape(b, s, gw)


def _fox_prompt_layer(x, g, w_in, b_f, w_out, final_g=None):
    b, s, d = x.shape
    x2 = x.reshape(b * s, d)
    nh = b_f.shape[0]
    w = nh * HEAD_DIM
    qkv, kv, gate, lf = _proj(x2, g, w_in, w, w, b_f)
    o = _flash_prompt(qkv.reshape(b, s, 3 * w), _fox_keybias(lf.reshape(b, s, nh)), min(FLASH_TILE, s),
                      name="fox_attn_prompt")
    y = _gated_out(x2, [o.reshape(b * s, w)], gate, w_out, final_g=final_g)
    return y.reshape(b, s, d), kv.reshape(b, s, 2, nh, HEAD_DIM), lf.reshape(b, s, nh)


def _dil_prompt_layer(x, g, w_in, w_out):
    b, s, d = x.shape
    x2 = x.reshape(b * s, d)
    ng = len(DIL_PATTERNS)
    wg = w_out.shape[0]
    gh = wg // HEAD_DIM
    wq = ng * wg
    qkv, kv, gate = _proj(x2, g, w_in, wq, wg)
    slopes = _alibi_slopes(ng * gh)
    qkv3 = qkv.reshape(b, s, 3 * wq)
    os_, lses = [], []
    for gi, (window, dil) in enumerate(DIL_PATTERNS):
        o, lse = _dil_attn_prompt(qkv3, slopes, gi, window, dil, gh)
        os_.append(o.reshape(b * s, wg))
        lses.append(lse.reshape(b * s, wg))
    y = _gated_out(x2, os_, gate, w_out, lses=lses)
    kv3 = kv.reshape(b, s, 2, ng, gh, HEAD_DIM)
    caches = [kv3[:, s - min(window, s):, :, gi] for gi, (window, _) in enumerate(DIL_PATTERNS)]
    return y.reshape(b, s, d), caches


def _moba_prompt_layer(x, g, w_in, w_out):
    b, s, d = x.shape
    x2 = x.reshape(b * s, d)
    w = w_out.shape[0]
    nh = w // HEAD_DIM
    qkv, kv, gate = _proj(x2, g, w_in, w, w)
    qkv3 = qkv.reshape(b, s, 3 * w)
    kmean = _moba_kmean(kv.reshape(b, s, 2 * w))
    t = MOBA_BLOCK
    sel = _moba_select(qkv3, kmean, t)
    o = _flash_prompt(qkv3, _moba_keyrows(s, _alibi_slopes(nh)), t, sel=sel, name="moba_attn_prompt")
    y = _gated_out(x2, [o.reshape(b * s, w)], gate, w_out)
    return y.reshape(b, s, d), kv.reshape(b, s, 2, nh, HEAD_DIM)


QROWS = 8


def _paged_kernel(pt_ref, *refs, mode, nh, npg, pg, nq):
    del pt_ref
    if mode == "fox":
        q_ref, kv_ref, lf_ref, kn_ref, vn_ref, lfn_ref, o_ref, m_sc, l_sc, acc_sc, carry = refs
    else:
        slope_ref, q_ref, kv_ref, kn_ref, vn_ref, o_ref, m_sc, l_sc, acc_sc, ksum = refs
    p = pl.program_id(1)
    past = npg * pg
    if mode == "fox":
        @pl.when(p == 0)
        def _():
            carry[...] = jnp.zeros_like(carry)

        r = lax.broadcasted_iota(jnp.int32, (pg, pg), 0)
        c = lax.broadcasted_iota(jnp.int32, (pg, pg), 1)
        cs = _dot3(lf_ref[...], (r <= c).astype(BF16)) + carry[...]
        carry[...] = cs[:, pg - 1:pg]
    else:
        ppb = MOBA_BLOCK // pg
        rowq = lax.broadcasted_iota(jnp.int32, (QROWS, pg), 0)
        colk = lax.broadcasted_iota(jnp.int32, (QROWS, pg), 1)
        dist = (past + rowq - (p * pg + colk)).astype(F32)

        @pl.when(p % ppb == 0)
        def _():
            for h in range(nh):
                ksum[h, p // ppb] = kv_ref[0, h]

        @pl.when(p % ppb != 0)
        def _():
            for h in range(nh):
                ksum[h, p // ppb] += kv_ref[0, h]

    scores = []
    for h in range(nh):
        s = jnp.dot(q_ref[0, h], kv_ref[0, h].astype(BF16), preferred_element_type=F32)
        if mode == "fox":
            scores.append(s - cs[h:h + 1, :])
        else:
            scores.append(s - slope_ref[h] * dist)
    s = jnp.concatenate(scores, axis=0)
    m = jnp.max(s, axis=1, keepdims=True)
    e = jnp.exp(s - m)
    l = jnp.sum(e, axis=1, keepdims=True)
    for h in range(nh):
        rows = slice(h * QROWS, (h + 1) * QROWS)
        m_sc[h, p] = m[rows]
        l_sc[h, p] = l[rows]
        acc_sc[h, p] = lax.dot_general(e[rows].astype(BF16), kv_ref[1, h].astype(BF16), _NT,
                                       preferred_element_type=F32)

    @pl.when(p == npg - 1)
    def _():
        rq = lax.broadcasted_iota(jnp.int32, (QROWS, QROWS), 0)
        cn = lax.broadcasted_iota(jnp.int32, (QROWS, QROWS), 1)
        new_ok = (cn <= rq) & (cn < nq)
        if mode == "fox":
            up8 = (rq <= cn).astype(BF16)
            csn = _dot3(lfn_ref[0], up8) + carry[...]
        scores_n = []
        for h in range(nh):
            sn = lax.dot_general(q_ref[0, h], kn_ref[0, h], _NT, preferred_element_type=F32)
            if mode == "fox":
                sn = sn - csn[h:h + 1, :]
            else:
                sn = sn - slope_ref[h] * (rq - cn).astype(F32)
            scores_n.append(jnp.where(new_ok, sn, MASKED))
        sn = jnp.concatenate(scores_n, axis=0)
        mn = jnp.max(sn, axis=1, keepdims=True)
        en = jnp.exp(sn - mn)
        ln = jnp.sum(en, axis=1, keepdims=True)
        if mode == "moba":
            nblk = npg // ppb
            km_lane = lax.broadcasted_iota(jnp.int32, (q_ref.shape[3], nblk), 1)
            gates = []
            for h in range(nh):
                kmean = jnp.zeros(km_lane.shape, F32)
                for n in range(nblk):
                    kmean = jnp.where(km_lane == n, jnp.sum(ksum[h, n], axis=1, keepdims=True), kmean)
                kmean = (kmean * (1.0 / MOBA_BLOCK)).astype(BF16)
                gates.append(jnp.dot(q_ref[0, h], kmean, preferred_element_type=F32))
            g = jnp.concatenate(gates, axis=0)
            blk_lane = lax.broadcasted_iota(jnp.int32, g.shape, 1)
            cnt = jnp.zeros(g.shape, jnp.int32)
            for m_ in range(nblk):
                gm = g[:, m_:m_ + 1]
                cnt = cnt + jnp.where((gm > g) | ((gm == g) & (m_ < blk_lane)), 1, 0)
            sel = (cnt < MOBA_TOPK) & (jnp.abs(g) < jnp.inf)
        for h in range(nh):
            rows = slice(h * QROWS, (h + 1) * QROWS)
            an = jnp.dot(en[rows].astype(BF16), vn_ref[0, h], preferred_element_type=F32)
            mp = m_sc[h]
            if mode == "moba":
                selp = jnp.stack([sel[rows, n // ppb:n // ppb + 1] for n in range(npg)], axis=0)
                mp = jnp.where(selp, mp, MASKED)
            mx = jnp.maximum(jnp.max(mp, axis=0), mn[rows])
            wp = jnp.exp(mp - mx)
            wn = jnp.exp(mn[rows] - mx)
            num = jnp.sum(wp * acc_sc[h], axis=0) + wn * an
            den = jnp.sum(wp * l_sc[h], axis=0) + wn * ln[rows]
            o_ref[0, h] = num / den


def _pad_rows(x):
    x = jnp.transpose(x, (0, 2, 1, 3))
    return jnp.pad(x, ((0, 0), (0, 0), (0, QROWS - x.shape[2]), (0, 0)))


def _paged_attn(mode, layer, page_table, qkv, cache_kv, nh, cache_lf=None, lf_new=None, slopes=None):
    db, tq, w3 = qkv.shape
    w = w3 // 3
    nl, nphys, pg = cache_kv.shape[:3]
    npg = page_table.shape[1]
    hd = HEAD_DIM
    split = lambda i: _pad_rows(qkv[:, :, i * w:(i + 1) * w].reshape(db, tq, nh, hd))
    q8, k8, v8 = split(0), split(1), split(2)
    pages = jnp.transpose(cache_kv, (0, 1, 3, 4, 5, 2))
    hblk = pl.BlockSpec((1, nh, QROWS, hd), lambda b, p, pt: (b, 0, 0, 0))
    in_specs = [hblk, pl.BlockSpec((None, None, 2, nh, hd, pg), lambda b, p, pt: (layer, pt[b, p], 0, 0, 0, 0))]
    args = [q8, pages]
    scratch = [pltpu.VMEM((nh, npg, QROWS, 1), F32), pltpu.VMEM((nh, npg, QROWS, 1), F32),
               pltpu.VMEM((nh, npg, QROWS, hd), F32)]
    if mode == "fox":
        in_specs.append(pl.BlockSpec((None, None, nh, pg), lambda b, p, pt: (layer, pt[b, p], 0, 0)))
        args.append(jnp.transpose(cache_lf, (0, 1, 3, 2)))
    in_specs += [hblk, hblk]
    args += [k8, v8]
    if mode == "fox":
        lfn = jnp.pad(jnp.transpose(lf_new, (0, 2, 1)), ((0, 0), (0, 0), (0, QROWS - tq)))
        in_specs.append(pl.BlockSpec((1, nh, QROWS), lambda b, p, pt: (b, 0, 0)))
        args.append(lfn)
        scratch.append(pltpu.VMEM((nh, 1), F32))
    else:
        in_specs = [pl.BlockSpec(memory_space=pltpu.SMEM)] + in_specs
        args = [slopes] + args
        scratch.append(pltpu.VMEM((nh, npg * pg // MOBA_BLOCK, hd, pg), F32))
    o8 = pl.pallas_call(
        functools.partial(_paged_kernel, mode=mode, nh=nh, npg=npg, pg=pg, nq=tq),
        grid_spec=pltpu.PrefetchScalarGridSpec(
            num_scalar_prefetch=1, grid=(db, npg), in_specs=in_specs, out_specs=hblk, scratch_shapes=scratch),
        out_shape=jax.ShapeDtypeStruct((db, nh, QROWS, hd), F32),
        compiler_params=_cparams(("arbitrary", "arbitrary")),
        name=mode + "_attn_sample",
    )(page_table, *args)
    return jnp.transpose(o8[:, :, :tq], (0, 2, 1, 3)).reshape(db, tq, w)


def _dil_step_kernel(slope_ref, q_ref, kn_ref, vn_ref, nf_ref, b0_ref, b1_ref, b2_ref,
                     o_ref, n0_ref, n1_ref, n2_ref, *, gh, nq):
    h = pl.program_id(1)
    rq = lax.broadcasted_iota(jnp.int32, (QROWS, QROWS), 0)
    cn = lax.broadcasted_iota(jnp.int32, (QROWS, QROWS), 1)
    jj = lax.broadcasted_iota(jnp.int32, (QROWS, LANES), 0)
    ll = lax.broadcasted_iota(jnp.int32, (QROWS, LANES), 1)
    place = ((ll == LANES - nq + jj) & (jj < nq)).astype(BF16)
    tail = lax.broadcasted_iota(jnp.int32, (q_ref.shape[-1], LANES), 1) >= LANES - nq
    tn = (((0,), (0,)), ((), ()))
    groups = list(zip(DIL_PATTERNS, (b0_ref, b1_ref, b2_ref), (n0_ref, n1_ref, n2_ref)))
    raw = [(jnp.dot(q_ref[0, gi, 0], buf[0, 0, 0].astype(BF16), preferred_element_type=F32),
            lax.dot_general(q_ref[0, gi, 0], kn_ref[0, gi, 0], _NT, preferred_element_type=F32))
           for gi, (_, buf, _) in enumerate(groups)]
    for gi, (_, buf, nbuf) in enumerate(groups):
        length = buf.shape[-1]
        for kv in range(2):
            rolled = pltpu.roll(buf[0, kv, 0], length - nq, axis=1)
            new_t = sum(lax.dot_general(a, place, tn, preferred_element_type=F32)
                        for a in _split3(nf_ref[0, kv, gi, 0]))
            if length > LANES:
                nbuf[0, kv, 0, :, :length - LANES] = rolled[:, :length - LANES]
            nbuf[0, kv, 0, :, length - LANES:] = jnp.where(tail, new_t, rolled[:, length - LANES:])
    probs, lses = [], []
    for gi, ((window, d), buf, _) in enumerate(groups):
        length = buf.shape[-1]
        slope = slope_ref[gi * gh + h]
        s, sn = raw[gi]
        t_i = lax.broadcasted_iota(jnp.int32, (QROWS, length), 0)
        l_i = lax.broadcasted_iota(jnp.int32, (QROWS, length), 1)
        ok = (l_i >= t_i) & (((l_i - t_i) & (d - 1)) == 0)
        s = jnp.where(ok, s - slope * (length + t_i - l_i).astype(F32), MASKED)
        okn = (cn <= rq) & (((rq - cn) & (d - 1)) == 0) & (cn < nq)
        sn = jnp.where(okn, sn - slope * (rq - cn).astype(F32), MASKED)
        m = jnp.maximum(jnp.max(s, axis=1, keepdims=True), jnp.max(sn, axis=1, keepdims=True))
        e = jnp.exp(s - m)
        en = jnp.exp(sn - m)
        den = jnp.sum(e, axis=1, keepdims=True) + jnp.sum(en, axis=1, keepdims=True)
        probs.append(((e / den).astype(BF16), (en / den).astype(BF16)))
        lses.append(m + jnp.log(den))
    outs = [lax.dot_general(p, buf[0, 1, 0].astype(BF16), _NT, preferred_element_type=F32)
            + jnp.dot(pn, vn_ref[0, gi, 0], preferred_element_type=F32)
            for gi, ((_, buf, _), (p, pn)) in enumerate(zip(groups, probs))]
    mx = functools.reduce(jnp.maximum, lses)
    es = [jnp.exp(l - mx) for l in lses]
    tot = functools.reduce(lambda a, b: a + b, es)
    o_ref[0, 0] = functools.reduce(lambda a, b: a + b, [(e / tot) * o for e, o in zip(es, outs)])


def _dil_step(qkv, kv_new, bufs, slopes, gh):
    db, tq, f3 = qkv.shape
    f = f3 // 3
    hd = HEAD_DIM
    ng = f // (gh * hd)
    split = lambda i: _pad_rows(qkv[:, :, i * f:(i + 1) * f].reshape(db, tq, ng * gh, hd)).reshape(db, ng, gh, QROWS, hd)
    q8, k8, v8 = split(0), split(1), split(2)
    nf = jnp.transpose(kv_new.reshape(db, tq, 2, ng, gh, hd), (0, 2, 3, 4, 1, 5))
    nf = jnp.pad(nf, ((0, 0),) * 4 + ((0, QROWS - tq), (0, 0)))
    hblk = pl.BlockSpec((1, ng, 1, QROWS, hd), lambda b, h: (b, 0, h, 0, 0))
    views = [jnp.transpose(buf, (0, 2, 3, 4, 1)) for buf in bufs]
    bspecs = [pl.BlockSpec((1, 2, 1, hd, v.shape[-1]), lambda b, h: (b, 0, h, 0, 0)) for v in views]
    res = pl.pallas_call(
        functools.partial(_dil_step_kernel, gh=gh, nq=tq),
        grid=(db, gh),
        in_specs=[pl.BlockSpec(memory_space=pltpu.SMEM), hblk, hblk, hblk,
                  pl.BlockSpec((1, 2, ng, 1, QROWS, hd), lambda b, h: (b, 0, 0, h, 0, 0))] + bspecs,
        out_specs=[pl.BlockSpec((1, 1, QROWS, hd), lambda b, h: (b, h, 0, 0))] + bspecs,
        out_shape=[jax.ShapeDtypeStruct((db, gh, QROWS, hd), F32)]
        + [jax.ShapeDtypeStruct(v.shape, v.dtype) for v in views],
        compiler_params=_cparams(("arbitrary", "arbitrary")),
        name="dil_step_sample",
    )(slopes, q8, k8, v8, nf, *views)
    o = jnp.transpose(res[0][:, :, :tq], (0, 2, 1, 3)).reshape(db, tq, gh * hd)
    return o, [jnp.transpose(r, (0, 4, 1, 2, 3)) for r in res[1:]]


def _fox_sample_layer(x, g, w_in, b_f, w_out, layer, cache_kv, cache_lf, page_table, final_g=None):
    db, tq, d = x.shape
    x2 = x.reshape(db * tq, d)
    nh = b_f.shape[0]
    w = nh * HEAD_DIM
    qkv, kv, gate, lf = _proj(x2, g, w_in, w, w, b_f)
    lf3 = lf.reshape(db, tq, nh)
    o = _paged_attn("fox", layer, page_table, qkv.reshape(db, tq, 3 * w), cache_kv, nh, cache_lf=cache_lf, lf_new=lf3)
    y = _gated_out(x2, [o.reshape(db * tq, w)], gate, w_out, final_g=final_g)
    return y.reshape(db, tq, d), kv.reshape(db, tq, 2, nh, HEAD_DIM), lf3


def _dil_sample_layer(x, g, w_in, w_out, bufs):
    db, tq, d = x.shape
    x2 = x.reshape(db * tq, d)
    ng = len(DIL_PATTERNS)
    wg = w_out.shape[0]
    gh = wg // HEAD_DIM
    wq = ng * wg
    qkv, kv, gate = _proj(x2, g, w_in, wq, wg)
    o, new_bufs = _dil_step(qkv.reshape(db, tq, 3 * wq), kv.reshape(db, tq, 2 * wq), bufs, _alibi_slopes(ng * gh), gh)
    y = _gated_out(x2, [o.reshape(db * tq, wg)], gate, w_out)
    return y.reshape(db, tq, d), new_bufs


def _moba_sample_layer(x, g, w_in, w_out, layer, cache_kv, page_table):
    db, tq, d = x.shape
    x2 = x.reshape(db * tq, d)
    w = w_out.shape[0]
    nh = w // HEAD_DIM
    qkv, kv, gate = _proj(x2, g, w_in, w, w)
    o = _paged_attn("moba", layer, page_table, qkv.reshape(db, tq, 3 * w), cache_kv, nh, slopes=_alibi_slopes(nh))
    y = _gated_out(x2, [o.reshape(db * tq, w)], gate, w_out)
    return y.reshape(db, tq, d), kv.reshape(db, tq, 2, nh, HEAD_DIM)


def kernel(x_prompt, x_sample, cache_fox_kv, cache_fox_logf, cache_dil_kv0, cache_dil_kv1, cache_dil_kv2, cache_moba_kv, page_table, fox_norm, fox_w_in, fox_b_f, fox_w_out, dil_norm, dil_w_in, dil_w_out, moba_norm, moba_w_in, moba_w_out, final_norm):
    depth = fox_norm.shape[0] + dil_norm.shape[0] + moba_norm.shape[0]
    dil_bufs = (cache_dil_kv0, cache_dil_kv1, cache_dil_kv2)
    xp, xs = x_prompt, x_sample
    fox_kv_p, fox_kv_s, fox_lf_p, fox_lf_s, moba_p, moba_s = [], [], [], [], [], []
    dil_p = [[] for _ in dil_bufs]
    dil_s = [[] for _ in dil_bufs]
    for i in range(depth):
        kind, j = i % 3, i // 3
        fg = final_norm if i == depth - 1 else None
        if kind == 0:
            xp, kv, lf = _fox_prompt_layer(xp, fox_norm[j], fox_w_in[j], fox_b_f[j], fox_w_out[j], fg)
            fox_kv_p.append(kv)
            fox_lf_p.append(lf)
            xs, kv, lf = _fox_sample_layer(xs, fox_norm[j], fox_w_in[j], fox_b_f[j], fox_w_out[j], j,
                                           cache_fox_kv, cache_fox_logf, page_table, fg)
            fox_kv_s.append(kv)
            fox_lf_s.append(lf)
        elif kind == 1:
            xp, caches = _dil_prompt_layer(xp, dil_norm[j], dil_w_in[j], dil_w_out[j])
            xs, bufs = _dil_sample_layer(xs, dil_norm[j], dil_w_in[j], dil_w_out[j], [b[j] for b in dil_bufs])
            for gi in range(len(dil_bufs)):
                dil_p[gi].append(caches[gi])
                dil_s[gi].append(bufs[gi])
        else:
            xp, kv = _moba_prompt_layer(xp, moba_norm[j], moba_w_in[j], moba_w_out[j])
            moba_p.append(kv)
            xs, kv = _moba_sample_layer(xs, moba_norm[j], moba_w_in[j], moba_w_out[j], j, cache_moba_kv, page_table)
            moba_s.append(kv)
    if depth % 3 != 1:
        raise NotImplementedError("final RMSNorm is fused into the last forgetting-attention layer")
    return (xp, xs,
            jnp.stack(fox_kv_p), jnp.stack(fox_kv_s), jnp.stack(fox_lf_p), jnp.stack(fox_lf_s),
            jnp.stack(dil_p[0]), jnp.stack(dil_s[0]), jnp.stack(dil_p[1]), jnp.stack(dil_s[1]),
            jnp.stack(dil_p[2]), jnp.stack(dil_s[2]),
            jnp.stack(moba_p), jnp.stack(moba_s))
```

```python
import functools

import jax
import jax.numpy as jnp
from jax import lax
from jax.experimental import pallas as pl
from jax.experimental.pallas import tpu as pltpu

F32 = jnp.float32
BF16 = jnp.bfloat16

HEAD_DIM = 64
LANES = 128
RMS_EPS = 1e-6
SCALE = HEAD_DIM ** -0.5
ALIBI_MAX = 8.0
MASKED = -1e30
VMEM_LIMIT = 56 * 1024 * 1024

DIL_PATTERNS = ((128, 1), (512, 4), (2048, 16))
MOBA_BLOCK = 256
MOBA_TOPK = 3
FLASH_TILE = 256

_NT = (((1,), (1,)), ((), ()))


def _cparams(sem):
    return pltpu.CompilerParams(dimension_semantics=sem, vmem_limit_bytes=VMEM_LIMIT)


def _alibi_slopes(n):
    return jnp.exp2(-ALIBI_MAX * jnp.arange(1, n + 1, dtype=F32) / n)


def _rms_bf16(x, g):
    ms = jnp.mean(x * x, axis=-1, keepdims=True)
    return ((x * lax.rsqrt(ms + RMS_EPS)) * g).astype(BF16)


def _split3(x):
    hi = x.astype(BF16)
    r1 = x - hi.astype(F32)
    mid = r1.astype(BF16)
    lo = (r1 - mid.astype(F32)).astype(BF16)
    return hi, mid, lo


def _dot3(x, m01):
    hi, mid, lo = _split3(x)
    d = lambda a: jnp.dot(a, m01, preferred_element_type=F32)
    return d(hi) + d(mid) + d(lo)


def _head_mask(lane, e):
    return (lane >= HEAD_DIM * e) & (lane < HEAD_DIM * (e + 1))


PROJ_CHUNK = 512


def _proj_kernel(x_ref, g_ref, w_ref, *rest, wq, wg, forget, feature_major, block_means):
    rest = list(rest)
    wt_ref = rest.pop(0) if feature_major else None
    if forget:
        wf_ref, bf_ref = rest.pop(0), rest.pop(0)
    qkv_ref, kv_ref, gate_ref = rest.pop(0), rest.pop(0), rest.pop(0)
    lf_ref = rest.pop(0) if forget else None
    km_ref = rest.pop(0) if block_means else None
    y = _rms_bf16(x_ref[...], g_ref[...])
    for c0 in range(0, 3 * wq + wg, PROJ_CHUNK):
        t = jnp.dot(y, w_ref[:, c0:c0 + PROJ_CHUNK], preferred_element_type=F32)
        if c0 < wq:
            qkv_ref[:, c0:c0 + PROJ_CHUNK] = (t * SCALE).astype(BF16)
        elif c0 < 3 * wq:
            qkv_ref[:, c0:c0 + PROJ_CHUNK] = t.astype(BF16)
            ck = slice(c0 - wq, c0 - wq + PROJ_CHUNK)
            if feature_major:
                kv_ref[0, ck, :] = lax.dot_general(wt_ref[ck, :], y, _NT, preferred_element_type=F32)
            else:
                kv_ref[:, ck] = t
            if block_means and c0 < 2 * wq:
                km_ref[0, :, ck] = jnp.mean(t.reshape(-1, MOBA_BLOCK, PROJ_CHUNK), axis=1)
        else:
            gate_ref[:, c0 - 3 * wq:c0 - 3 * wq + PROJ_CHUNK] = t
    if forget:
        fl = jnp.dot(y, wf_ref[...], preferred_element_type=F32)
        lf_ref[...] = jax.nn.log_sigmoid(fl + bf_ref[...])


def _proj(x2, g, w_in, wq, wg, b_f=None, seq=None, block_means=False):
    n, d = x2.shape
    wmain = 3 * wq + wg
    tm = min(512, n)
    wb = w_in.astype(BF16)
    row = lambda i: (i, 0)
    fixed = lambda i: (0, 0)
    once = dict(pipeline_mode=pl.Buffered(1))
    in_specs = [pl.BlockSpec((tm, d), row), pl.BlockSpec((1, d), fixed), pl.BlockSpec((d, wmain), fixed, **once)]
    args = [x2, g.reshape(1, d), wb[:, :wmain]]
    out_specs = [pl.BlockSpec((tm, 3 * wq), row)]
    out_shape = [jax.ShapeDtypeStruct((n, 3 * wq), BF16)]
    if seq is not None:
        spb = seq // tm
        in_specs.append(pl.BlockSpec((2 * wq, d), fixed, **once))
        args.append(jnp.transpose(wb[:, wq:3 * wq]))
        out_specs.append(pl.BlockSpec((1, 2 * wq, tm), lambda i: (i // spb, 0, i % spb)))
        out_shape.append(jax.ShapeDtypeStruct((n // seq, 2 * wq, seq), F32))
    else:
        out_specs.append(pl.BlockSpec((tm, 2 * wq), row))
        out_shape.append(jax.ShapeDtypeStruct((n, 2 * wq), F32))
    out_specs.append(pl.BlockSpec((tm, wg), row))
    out_shape.append(jax.ShapeDtypeStruct((n, wg), F32))
    if b_f is not None:
        nh = b_f.shape[0]
        in_specs += [pl.BlockSpec((d, nh), fixed), pl.BlockSpec((1, nh), fixed)]
        args += [wb[:, wmain:], b_f.reshape(1, nh)]
        out_specs.append(pl.BlockSpec((tm, nh), row))
        out_shape.append(jax.ShapeDtypeStruct((n, nh), F32))
    if block_means:
        out_specs.append(pl.BlockSpec((1, tm // MOBA_BLOCK, wq), lambda i: (i, 0, 0)))
        out_shape.append(jax.ShapeDtypeStruct((n // tm, tm // MOBA_BLOCK, wq), F32))
    return pl.pallas_call(
        functools.partial(_proj_kernel, wq=wq, wg=wg, forget=b_f is not None, feature_major=seq is not None,
                          block_means=block_means),
        grid=(n // tm,),
        in_specs=in_specs, out_specs=out_specs, out_shape=out_shape,
        compiler_params=_cparams(("arbitrary",)),
        name="norm_proj",
    )(*args)


def _out_kernel(x_ref, gate_ref, w_ref, *rest, n_o, final):
    o_refs = rest[:n_o]
    rest = rest[n_o:]
    if n_o == 1:
        o = o_refs[0][...]
    else:
        lse_refs, rest = rest[:n_o], rest[n_o:]
        lses = [r[...] for r in lse_refs]
        m = functools.reduce(jnp.maximum, lses)
        es = [jnp.exp(l - m) for l in lses]
        den = functools.reduce(lambda a, b: a + b, es)
        o = functools.reduce(lambda a, b: a + b, [(e / den) * r[...] for e, r in zip(es, o_refs)])
    h = (o * jax.nn.silu(gate_ref[...])).astype(BF16)
    y = x_ref[...] + jnp.dot(h, w_ref[...], preferred_element_type=F32)
    if final:
        fg_ref, y_ref = rest
        ms = jnp.mean(y * y, axis=-1, keepdims=True)
        y_ref[...] = (y * lax.rsqrt(ms + RMS_EPS)) * fg_ref[...]
    else:
        rest[0][...] = y


def _gated_out(x2, os_, gate2, w_out, lses=(), final_g=None):
    n, d = x2.shape
    w = gate2.shape[1]
    tm = min(512, n)
    row = lambda i: (i, 0)
    fixed = lambda i: (0, 0)
    in_specs = [pl.BlockSpec((tm, d), row), pl.BlockSpec((tm, w), row), pl.BlockSpec((w, d), fixed)]
    in_specs += [pl.BlockSpec((tm, w), row)] * (len(os_) + len(lses))
    args = [x2, gate2, w_out.astype(BF16), *os_, *lses]
    if final_g is not None:
        in_specs.append(pl.BlockSpec((1, d), fixed))
        args.append(final_g.reshape(1, d))
    return pl.pallas_call(
        functools.partial(_out_kernel, n_o=len(os_), final=final_g is not None),
        grid=(n // tm,),
        in_specs=in_specs,
        out_specs=pl.BlockSpec((tm, d), row),
        out_shape=jax.ShapeDtypeStruct((n, d), F32),
        compiler_params=_cparams(("arbitrary",)),
        name="gated_out",
    )(*args)


def _place3(parts, nh):
    rr = lax.broadcasted_iota(jnp.int32, (nh, LANES), 0)
    ll = lax.broadcasted_iota(jnp.int32, (nh, LANES), 1)
    return sum(jnp.dot(p, (ll == 3 * rr + k).astype(BF16), preferred_element_type=F32) for k, p in enumerate(parts))


def _fox_keybias_kernel(lf_ref, o_ref, carry, *, cb, nh):
    @pl.when(pl.program_id(1) == 0)
    def _():
        carry[...] = jnp.zeros_like(carry)

    r = lax.broadcasted_iota(jnp.int32, (cb, cb), 0)
    c = lax.broadcasted_iota(jnp.int32, (cb, cb), 1)
    lower = (c <= r).astype(BF16)
    cs = sum(jnp.dot(lower, p, preferred_element_type=F32) for p in _split3(lf_ref[0])) + carry[...]
    carry[...] = cs[cb - 1:cb, :]
    o_ref[0] = _place3(_split3(-cs), nh).astype(BF16)


def _fox_keybias(lf):
    b, s, nh = lf.shape
    cb = min(512, s)
    return pl.pallas_call(
        functools.partial(_fox_keybias_kernel, cb=cb, nh=nh),
        grid=(b, s // cb),
        in_specs=[pl.BlockSpec((1, cb, nh), lambda i, j: (i, j, 0))],
        out_specs=pl.BlockSpec((1, cb, LANES), lambda i, j: (i, j, 0)),
        out_shape=jax.ShapeDtypeStruct((b, s, LANES), BF16),
        scratch_shapes=[pltpu.VMEM((1, nh), F32)],
        compiler_params=_cparams(("arbitrary", "arbitrary")),
        name="fox_keybias",
    )(lf)


def _moba_keyrows_kernel(slope_ref, o_ref, *, cb, nh):
    base = pl.program_id(0) * cb
    pos = base + lax.broadcasted_iota(jnp.int32, (cb, nh), 0)
    placed = _place3(_split3(pos.astype(F32) * slope_ref[...]), nh)
    posl = base + lax.broadcasted_iota(jnp.int32, (cb, LANES), 0)
    lane = lax.broadcasted_iota(jnp.int32, (cb, LANES), 1)
    onehot = jnp.where(lane - LANES // 2 == posl // MOBA_BLOCK, 1.0, 0.0)
    o_ref[0] = (placed + onehot).astype(BF16)


def _moba_keyrows(s, slopes):
    nh = slopes.shape[0]
    cb = min(512, s)
    return pl.pallas_call(
        functools.partial(_moba_keyrows_kernel, cb=cb, nh=nh),
        grid=(s // cb,),
        in_specs=[pl.BlockSpec((1, nh), lambda j: (0, 0))],
        out_specs=pl.BlockSpec((1, cb, LANES), lambda j: (0, j, 0)),
        out_shape=jax.ShapeDtypeStruct((1, s, LANES), BF16),
        compiler_params=_cparams(("arbitrary",)),
        name="moba_keyrows",
    )(slopes.reshape(1, nh))


def _flash_kernel(*refs, t, nt, masked_blocks):
    if masked_blocks:
        q_ref, k_ref, kx_ref, v_ref, sel_ref, o_ref, vt_sc, rhs_sc = refs
    else:
        q_ref, k_ref, kx_ref, v_ref, o_ref, vt_sc, rhs_sc = refs
    hp = pl.program_id(1)
    qi = pl.program_id(2)
    half = LANES // 2

    @pl.when(qi == 0)
    def _():
        for j in range(nt):
            vt_sc[j] = jnp.transpose(v_ref[0, j * t:(j + 1) * t, :].astype(F32)).astype(BF16)

    q_t = jnp.transpose(q_ref[0].astype(F32))
    frow = lax.broadcasted_iota(jnp.int32, (LANES, t), 0)
    xrow = lax.broadcasted_iota(jnp.int32, (half, t), 0)
    for e in range(2):
        h = 2 * hp + e
        qm = jnp.where(_head_mask(frow, e), q_t, 0.0).astype(BF16)
        ones = jnp.where((xrow >= 3 * h) & (xrow < 3 * h + 3), 1.0, 0.0).astype(BF16)
        low = sel_ref[0, 0, e] if masked_blocks else jnp.zeros((half, t), BF16)
        rhs_sc[e] = jnp.concatenate([qm, ones, low], axis=0)
    krow = lax.broadcasted_iota(jnp.int32, (t, t), 0)
    qcol = lax.broadcasted_iota(jnp.int32, (t, t), 1)

    def scores(j):
        ks = pl.multiple_of(j * t, t)
        kc = jnp.concatenate([k_ref[0, pl.ds(ks, t), :], kx_ref[0, pl.ds(ks, t), :]], axis=1)
        return tuple(jnp.dot(kc, rhs_sc[e], preferred_element_type=F32) for e in range(2))

    def update(j, s2, state, diagonal):
        vt = vt_sc[j]
        new = []
        for e in range(2):
            m_prev, l_prev, acc = state[e]
            s = jnp.where(krow <= qcol, s2[e], MASKED) if diagonal else s2[e]
            m_new = jnp.maximum(m_prev, jnp.max(s, axis=0, keepdims=True))
            alpha = jnp.exp(m_prev - m_new)
            p = jnp.exp(s - m_new)
            l_new = alpha * l_prev + jnp.sum(p, axis=0, keepdims=True)
            pv = jnp.dot(vt[e * half:(e + 1) * half, :], p.astype(BF16), preferred_element_type=F32)
            new.append((m_new, l_new, alpha * acc + pv))
        return tuple(new)

    def body(j, carry):
        s2, state = carry
        s_next = scores(j + 1)
        return s_next, update(j, s2, state, False)

    init = tuple((jnp.full((1, t), MASKED, F32), jnp.zeros((1, t), F32), jnp.zeros((half, t), F32))
                 for _ in range(2))
    s2, state = lax.fori_loop(0, qi, body, (scores(0), init))
    state = update(qi, s2, state, True)
    o_t = jnp.concatenate([acc / l for _, l, acc in state], axis=0)
    o_ref[0] = jnp.transpose(o_t)


def _flash_prompt(qkv, keyrows, t, sel=None, name="attn_prompt"):
    b, s, w3 = qkv.shape
    w = w3 // 3
    hp = w // LANES
    nt = s // t
    per_batch = keyrows.shape[0] > 1
    in_specs = [pl.BlockSpec((1, t, LANES), lambda i, h, j: (i, j, h)),
                pl.BlockSpec((1, s, LANES), lambda i, h, j: (i, 0, hp + h)),
                pl.BlockSpec((1, s, LANES), lambda i, h, j: (i if per_batch else 0, 0, 0)),
                pl.BlockSpec((1, s, LANES), lambda i, h, j: (i, 0, 2 * hp + h))]
    args = [qkv, qkv, keyrows, qkv]
    if sel is not None:
        in_specs.append(pl.BlockSpec((1, 1, 2, LANES // 2, t), lambda i, h, j: (i, h, 0, 0, j)))
        args.append(sel)
    return pl.pallas_call(
        functools.partial(_flash_kernel, t=t, nt=nt, masked_blocks=sel is not None),
        grid=(b, hp, nt),
        in_specs=in_specs,
        out_specs=pl.BlockSpec((1, t, LANES), lambda i, h, j: (i, j, h)),
        out_shape=jax.ShapeDtypeStruct((b, s, w), F32),
        scratch_shapes=[pltpu.VMEM((nt, LANES, t), BF16), pltpu.VMEM((2, 2 * LANES, t), BF16)],
        compiler_params=_cparams(("arbitrary", "arbitrary", "arbitrary")),
        name=name,
    )(*args)


def _topk_rank_mask(g, n_idx, own):
    nb = g.shape[0]
    cnt = jnp.zeros(g.shape, jnp.int32)
    for m in range(nb):
        gm = g[m:m + 1, :]
        beats = (gm > g) | ((gm == g) & (m < n_idx))
        cnt = cnt + jnp.where(beats & (m < own), 1, 0)
    return (n_idx < own) & (cnt < MOBA_TOPK) & (jnp.abs(g) < jnp.inf)


def _moba_select_kernel(q_ref, km_ref, o_ref, *, t, nb):
    qi = pl.program_id(2)
    q = q_ref[0]
    km = km_ref[0].astype(BF16)
    klane = lax.broadcasted_iota(jnp.int32, (nb, LANES), 1)
    n_idx = lax.broadcasted_iota(jnp.int32, (nb, t), 0)
    own = (qi * t + lax.broadcasted_iota(jnp.int32, (1, t), 1)) // MOBA_BLOCK
    for e in range(2):
        kme = jnp.where(_head_mask(klane, e), km, jnp.zeros_like(km))
        g = lax.dot_general(kme, q, _NT, preferred_element_type=F32)
        keep = _topk_rank_mask(g, n_idx, own) | (n_idx == own)
        bias = jnp.where(keep, 0.0, MASKED)
        if nb < LANES // 2:
            bias = jnp.concatenate([bias, jnp.zeros((LANES // 2 - nb, t), F32)], axis=0)
        o_ref[0, 0, e] = bias.astype(BF16)


def _moba_select(qkv, kmean, t):
    b, s, w3 = qkv.shape
    w = w3 // 3
    hp = w // LANES
    nb = kmean.shape[1]
    return pl.pallas_call(
        functools.partial(_moba_select_kernel, t=t, nb=nb),
        grid=(b, hp, s // t),
        in_specs=[pl.BlockSpec((1, t, LANES), lambda i, h, j: (i, j, h)),
                  pl.BlockSpec((1, nb, LANES), lambda i, h, j: (i, 0, h))],
        out_specs=pl.BlockSpec((1, 1, 2, LANES // 2, t), lambda i, h, j: (i, h, 0, 0, j)),
        out_shape=jax.ShapeDtypeStruct((b, hp, 2, LANES // 2, s), BF16),
        compiler_params=_cparams(("arbitrary", "arbitrary", "arbitrary")),
        name="moba_select",
    )(qkv, kmean)


def _dil_attn_kernel(slope_ref, q_ref, kp_ref, kc_ref, vp_ref, vc_ref, o_ref, lse_ref, *, t, d, g, gh):
    ut = pl.program_id(2)
    lane = lax.broadcasted_iota(jnp.int32, (t, LANES), 1)
    row = lax.broadcasted_iota(jnp.int32, (t, t), 0)
    col = lax.broadcasted_iota(jnp.int32, (t, t), 1)
    jc = row - col
    jp = jc + t
    ok_c = jc >= 0
    ok_p = (jp <= t) & (ut > 0)
    dist_c = (jc * d).astype(F32)
    dist_p = (jp * d).astype(F32)
    heads = [(pair, e) for pair in range(gh // 2) for e in range(2)]
    pair_lanes = lambda pair: slice(pair * LANES, (pair + 1) * LANES)
    raw = []
    for pair, e in heads:
        sl = pair_lanes(pair)
        q = q_ref[0, :, sl]
        qm = jnp.where(_head_mask(lane, e), q, jnp.zeros_like(q))
        raw.append((lax.dot_general(qm, kp_ref[0, :, sl], _NT, preferred_element_type=F32),
                    lax.dot_general(qm, kc_ref[0, :, sl], _NT, preferred_element_type=F32)))
    probs, lses = [], []
    for (pair, e), (sp, sc) in zip(heads, raw):
        slope = slope_ref[g * gh + 2 * pair + e]
        sp = jnp.where(ok_p, sp - slope * dist_p, MASKED)
        sc = jnp.where(ok_c, sc - slope * dist_c, MASKED)
        m = jnp.maximum(jnp.max(sp, axis=1, keepdims=True), jnp.max(sc, axis=1, keepdims=True))
        ep = jnp.exp(sp - m)
        ec = jnp.exp(sc - m)
        den = jnp.sum(ep, axis=1, keepdims=True) + jnp.sum(ec, axis=1, keepdims=True)
        probs.append(((ep / den).astype(BF16), (ec / den).astype(BF16)))
        lses.append(m + jnp.log(den))
    outs = [jnp.dot(pp, vp_ref[0, :, pair_lanes(pair)], preferred_element_type=F32)
            + jnp.dot(pc, vc_ref[0, :, pair_lanes(pair)], preferred_element_type=F32)
            for (pair, e), (pp, pc) in zip(heads, probs)]
    for pair in range(gh // 2):
        sl = pair_lanes(pair)
        o_ref[0, :, sl] = jnp.where(lane < HEAD_DIM, outs[2 * pair], outs[2 * pair + 1])
        lse_ref[0, :, sl] = jnp.where(lane < HEAD_DIM, lses[2 * pair], lses[2 * pair + 1])


def _dil_attn_prompt(qkv, slopes, g, window, d, gh):
    b, s, f3 = qkv.shape
    gw = gh * HEAD_DIM
    nblk = f3 // gw
    per = nblk // 3
    t = window // d
    su = s // d
    qv = qkv.reshape(b, su, d * f3)
    cur = lambda sec: (lambda i, r, u: (i, u, r * nblk + sec * per + g))
    prev = lambda sec: (lambda i, r, u: (i, jnp.maximum(u - 1, 0), r * nblk + sec * per + g))
    blk = (1, t, gw)
    o, lse = pl.pallas_call(
        functools.partial(_dil_attn_kernel, t=t, d=d, g=g, gh=gh),
        grid=(b, d, su // t),
        in_specs=[pl.BlockSpec(memory_space=pltpu.SMEM),
                  pl.BlockSpec(blk, cur(0)), pl.BlockSpec(blk, prev(1)), pl.BlockSpec(blk, cur(1)),
                  pl.BlockSpec(blk, prev(2)), pl.BlockSpec(blk, cur(2))],
        out_specs=[pl.BlockSpec(blk, lambda i, r, u: (i, u, r))] * 2,
        out_shape=[jax.ShapeDtypeStruct((b, su, d * gw), F32)] * 2,
        compiler_params=_cparams(("arbitrary", "arbitrary", "arbitrary")),
        name="dil_attn_prompt",
    )(slopes, qv, qv, qv, qv, qv)
    return o.reshape(b, s, gw), lse.reshape(b, s, gw)


def _fox_prompt_layer(x, g, w_in, b_f, w_out, final_g=None):
    b, s, d = x.shape
    x2 = x.reshape(b * s, d)
    nh = b_f.shape[0]
    w = nh * HEAD_DIM
    qkv, kv_t, gate, lf = _proj(x2, g, w_in, w, w, b_f, seq=s)
    o = _flash_prompt(qkv.reshape(b, s, 3 * w), _fox_keybias(lf.reshape(b, s, nh)), min(FLASH_TILE, s),
                      name="fox_attn_prompt")
    y = _gated_out(x2, [o.reshape(b * s, w)], gate, w_out, final_g=final_g)
    return y.reshape(b, s, d), _token_major(kv_t, nh), lf.reshape(b, s, nh)


def _dil_prompt_layer(x, g, w_in, w_out):
    b, s, d = x.shape
    x2 = x.reshape(b * s, d)
    ng = len(DIL_PATTERNS)
    wg = w_out.shape[0]
    gh = wg // HEAD_DIM
    wq = ng * wg
    qkv, kv_t, gate = _proj(x2, g, w_in, wq, wg, seq=s)
    slopes = _alibi_slopes(ng * gh)
    qkv3 = qkv.reshape(b, s, 3 * wq)
    os_, lses = [], []
    for gi, (window, dil) in enumerate(DIL_PATTERNS):
        o, lse = _dil_attn_prompt(qkv3, slopes, gi, window, dil, gh)
        os_.append(o.reshape(b * s, wg))
        lses.append(lse.reshape(b * s, wg))
    y = _gated_out(x2, os_, gate, w_out, lses=lses)
    kv6 = kv_t.reshape(b, 2, ng, gh * HEAD_DIM, s)
    caches = [_token_major(kv6[:, :, gi, :, s - min(window, s):].reshape(b, 2 * gh * HEAD_DIM, -1), gh)
              for gi, (window, _) in enumerate(DIL_PATTERNS)]
    return y.reshape(b, s, d), caches


def _token_major(kv_t, nh):
    b, _, length = kv_t.shape
    return jnp.transpose(kv_t.reshape(b, 2, nh, HEAD_DIM, length), (0, 4, 1, 2, 3))


def _moba_prompt_layer(x, g, w_in, w_out):
    b, s, d = x.shape
    x2 = x.reshape(b * s, d)
    w = w_out.shape[0]
    nh = w // HEAD_DIM
    qkv, kv_t, gate, kmean = _proj(x2, g, w_in, w, w, seq=s, block_means=True)
    qkv3 = qkv.reshape(b, s, 3 * w)
    t = MOBA_BLOCK
    sel = _moba_select(qkv3, kmean.reshape(b, s // MOBA_BLOCK, w), t)
    o = _flash_prompt(qkv3, _moba_keyrows(s, _alibi_slopes(nh)), t, sel=sel, name="moba_attn_prompt")
    y = _gated_out(x2, [o.reshape(b * s, w)], gate, w_out)
    return y.reshape(b, s, d), _token_major(kv_t, nh)


QROWS = 8


PAGES_PER_STEP = MOBA_BLOCK // 128


def _paged_kernel(pt_ref, *refs, mode, nh, npg, pg, nq):
    del pt_ref
    pps = PAGES_PER_STEP
    if mode == "fox":
        q_ref, kv_refs, lf_refs = refs[0], refs[1:1 + pps], refs[1 + pps:1 + 2 * pps]
        kn_ref, vn_ref, lfn_ref, o_ref, m_sc, l_sc, acc_sc, carry = refs[1 + 2 * pps:]
    else:
        slope_ref, q_ref, kv_refs = refs[0], refs[1], refs[2:2 + pps]
        kn_ref, vn_ref, o_ref, m_sc, l_sc, acc_sc, ksum = refs[2 + pps:]
    step = pl.program_id(1)
    past = npg * pg
    ppb = MOBA_BLOCK // pg
    if mode == "fox":
        @pl.when(step == 0)
        def _():
            carry[...] = jnp.zeros_like(carry)

        r = lax.broadcasted_iota(jnp.int32, (pg, pg), 0)
        c = lax.broadcasted_iota(jnp.int32, (pg, pg), 1)
        upper = (r <= c).astype(BF16)
        cs = []
        run = carry[...]
        for lf_ref in lf_refs:
            cs.append(_dot3(lf_ref[...], upper) + run)
            run = cs[-1][:, pg - 1:pg]
        carry[...] = run
    else:
        rowq = lax.broadcasted_iota(jnp.int32, (QROWS, pg), 0)
        colk = lax.broadcasted_iota(jnp.int32, (QROWS, pg), 1)
        for h in range(nh):
            ksum[h, step] = functools.reduce(lambda a, b: a + b, [kv_ref[0, h] for kv_ref in kv_refs])

    scores = []
    for i, kv_ref in enumerate(kv_refs):
        if mode == "moba":
            dist = (past + rowq - ((step * pps + i) * pg + colk)).astype(F32)
        for h in range(nh):
            s = jnp.dot(q_ref[0, h], kv_ref[0, h].astype(BF16), preferred_element_type=F32)
            if mode == "fox":
                scores.append(s - cs[i][h:h + 1, :])
            else:
                scores.append(s - slope_ref[h] * dist)
    s = jnp.concatenate(scores, axis=0)
    m = jnp.max(s, axis=1, keepdims=True)
    e = jnp.exp(s - m)
    l = jnp.sum(e, axis=1, keepdims=True)
    for i, kv_ref in enumerate(kv_refs):
        p = step * pps + i
        for h in range(nh):
            rows = slice((i * nh + h) * QROWS, (i * nh + h + 1) * QROWS)
            m_sc[h, p] = m[rows]
            l_sc[h, p] = l[rows]
            acc_sc[h, p] = lax.dot_general(e[rows].astype(BF16), kv_ref[1, h].astype(BF16), _NT,
                                           preferred_element_type=F32)

    @pl.when(step == npg // pps - 1)
    def _():
        rq = lax.broadcasted_iota(jnp.int32, (QROWS, QROWS), 0)
        cn = lax.broadcasted_iota(jnp.int32, (QROWS, QROWS), 1)
        new_ok = (cn <= rq) & (cn < nq)
        if mode == "fox":
            up8 = (rq <= cn).astype(BF16)
            csn = _dot3(lfn_ref[0], up8) + carry[...]
        scores_n = []
        for h in range(nh):
            sn = lax.dot_general(q_ref[0, h], kn_ref[0, h], _NT, preferred_element_type=F32)
            if mode == "fox":
                sn = sn - csn[h:h + 1, :]
            else:
                sn = sn - slope_ref[h] * (rq - cn).astype(F32)
            scores_n.append(jnp.where(new_ok, sn, MASKED))
        sn = jnp.concatenate(scores_n, axis=0)
        mn = jnp.max(sn, axis=1, keepdims=True)
        en = jnp.exp(sn - mn)
        ln = jnp.sum(en, axis=1, keepdims=True)
        if mode == "moba":
            nblk = npg // ppb
            km_lane = lax.broadcasted_iota(jnp.int32, (q_ref.shape[3], nblk), 1)
            gates = []
            for h in range(nh):
                kmean = jnp.zeros(km_lane.shape, F32)
                for n in range(nblk):
                    kmean = jnp.where(km_lane == n, jnp.sum(ksum[h, n], axis=1, keepdims=True), kmean)
                kmean = (kmean * (1.0 / MOBA_BLOCK)).astype(BF16)
                gates.append(jnp.dot(q_ref[0, h], kmean, preferred_element_type=F32))
            g = jnp.concatenate(gates, axis=0)
            blk_lane = lax.broadcasted_iota(jnp.int32, g.shape, 1)
            cnt = jnp.zeros(g.shape, jnp.int32)
            for m_ in range(nblk):
                gm = g[:, m_:m_ + 1]
                cnt = cnt + jnp.where((gm > g) | ((gm == g) & (m_ < blk_lane)), 1, 0)
            sel = (cnt < MOBA_TOPK) & (jnp.abs(g) < jnp.inf)
        for h in range(nh):
            rows = slice(h * QROWS, (h + 1) * QROWS)
            an = jnp.dot(en[rows].astype(BF16), vn_ref[0, h], preferred_element_type=F32)
            mp = m_sc[h]
            if mode == "moba":
                selp = jnp.stack([sel[rows, n // ppb:n // ppb + 1] for n in range(npg)], axis=0)
                mp = jnp.where(selp, mp, MASKED)
            mx = jnp.maximum(jnp.max(mp, axis=0), mn[rows])
            wp = jnp.exp(mp - mx)
            wn = jnp.exp(mn[rows] - mx)
            num = jnp.sum(wp * acc_sc[h], axis=0) + wn * an
            den = jnp.sum(wp * l_sc[h], axis=0) + wn * ln[rows]
            o_ref[0, h] = num / den


def _pad_rows(x):
    x = jnp.transpose(x, (0, 2, 1, 3))
    return jnp.pad(x, ((0, 0), (0, 0), (0, QROWS - x.shape[2]), (0, 0)))


def _paged_attn(mode, layer, page_table, qkv, cache_kv, nh, cache_lf=None, lf_new=None, slopes=None):
    db, tq, w3 = qkv.shape
    w = w3 // 3
    nl, nphys, pg = cache_kv.shape[:3]
    npg = page_table.shape[1]
    hd = HEAD_DIM
    split = lambda i: _pad_rows(qkv[:, :, i * w:(i + 1) * w].reshape(db, tq, nh, hd))
    q8, k8, v8 = split(0), split(1), split(2)
    pages = jnp.transpose(cache_kv, (0, 1, 3, 4, 5, 2))
    hblk = pl.BlockSpec((1, nh, QROWS, hd), lambda b, p, pt: (b, 0, 0, 0))
    pps = PAGES_PER_STEP
    assert npg % pps == 0 and pg * pps == MOBA_BLOCK
    page = lambda i: (lambda b, p, pt: (layer, pt[b, p * pps + i], 0, 0, 0, 0))
    in_specs = [hblk] + [pl.BlockSpec((None, None, 2, nh, hd, pg), page(i)) for i in range(pps)]
    args = [q8] + [pages] * pps
    scratch = [pltpu.VMEM((nh, npg, QROWS, 1), F32), pltpu.VMEM((nh, npg, QROWS, 1), F32),
               pltpu.VMEM((nh, npg, QROWS, hd), F32)]
    if mode == "fox":
        lf_page = lambda i: (lambda b, p, pt: (layer, pt[b, p * pps + i], 0, 0))
        in_specs += [pl.BlockSpec((None, None, nh, pg), lf_page(i)) for i in range(pps)]
        args += [jnp.transpose(cache_lf, (0, 1, 3, 2))] * pps
    in_specs += [hblk, hblk]
    args += [k8, v8]
    if mode == "fox":
        lfn = jnp.pad(jnp.transpose(lf_new, (0, 2, 1)), ((0, 0), (0, 0), (0, QROWS - tq)))
        in_specs.append(pl.BlockSpec((1, nh, QROWS), lambda b, p, pt: (b, 0, 0)))
        args.append(lfn)
        scratch.append(pltpu.VMEM((nh, 1), F32))
    else:
        in_specs = [pl.BlockSpec(memory_space=pltpu.SMEM)] + in_specs
        args = [slopes] + args
        scratch.append(pltpu.VMEM((nh, npg * pg // MOBA_BLOCK, hd, pg), F32))
    o8 = pl.pallas_call(
        functools.partial(_paged_kernel, mode=mode, nh=nh, npg=npg, pg=pg, nq=tq),
        grid_spec=pltpu.PrefetchScalarGridSpec(
            num_scalar_prefetch=1, grid=(db, npg // pps), in_specs=in_specs, out_specs=hblk, scratch_shapes=scratch),
        out_shape=jax.ShapeDtypeStruct((db, nh, QROWS, hd), F32),
        compiler_params=_cparams(("arbitrary", "arbitrary")),
        name=mode + "_attn_sample",
    )(page_table, *args)
    return jnp.transpose(o8[:, :, :tq], (0, 2, 1, 3)).reshape(db, tq, w)


def _dil_step_kernel(slope_ref, q_ref, kn_ref, vn_ref, nf_ref, b0_ref, b1_ref, b2_ref,
                     o_ref, n0_ref, n1_ref, n2_ref, *, gh, nq):
    h = pl.program_id(1)
    rq = lax.broadcasted_iota(jnp.int32, (QROWS, QROWS), 0)
    cn = lax.broadcasted_iota(jnp.int32, (QROWS, QROWS), 1)
    jj = lax.broadcasted_iota(jnp.int32, (QROWS, LANES), 0)
    ll = lax.broadcasted_iota(jnp.int32, (QROWS, LANES), 1)
    place = ((ll == LANES - nq + jj) & (jj < nq)).astype(BF16)
    tail = lax.broadcasted_iota(jnp.int32, (q_ref.shape[-1], LANES), 1) >= LANES - nq
    tn = (((0,), (0,)), ((), ()))
    groups = list(zip(DIL_PATTERNS, (b0_ref, b1_ref, b2_ref), (n0_ref, n1_ref, n2_ref)))
    raw = [(jnp.dot(q_ref[0, gi, 0], buf[0, 0, 0].astype(BF16), preferred_element_type=F32),
            lax.dot_general(q_ref[0, gi, 0], kn_ref[0, gi, 0], _NT, preferred_element_type=F32))
           for gi, (_, buf, _) in enumerate(groups)]
    for gi, (_, buf, nbuf) in enumerate(groups):
        length = buf.shape[-1]
        for kv in range(2):
            rolled = pltpu.roll(buf[0, kv, 0], length - nq, axis=1)
            new_t = sum(lax.dot_general(a, place, tn, preferred_element_type=F32)
                        for a in _split3(nf_ref[0, kv, gi, 0]))
            if length > LANES:
                nbuf[0, kv, 0, :, :length - LANES] = rolled[:, :length - LANES]
            nbuf[0, kv, 0, :, length - LANES:] = jnp.where(tail, new_t, rolled[:, length - LANES:])
    probs, lses = [], []
    for gi, ((window, d), buf, _) in enumerate(groups):
        length = buf.shape[-1]
        slope = slope_ref[gi * gh + h]
        s, sn = raw[gi]
        t_i = lax.broadcasted_iota(jnp.int32, (QROWS, length), 0)
        l_i = lax.broadcasted_iota(jnp.int32, (QROWS, length), 1)
        ok = (l_i >= t_i) & (((l_i - t_i) & (d - 1)) == 0)
        s = jnp.where(ok, s - slope * (length + t_i - l_i).astype(F32), MASKED)
        okn = (cn <= rq) & (((rq - cn) & (d - 1)) == 0) & (cn < nq)
        sn = jnp.where(okn, sn - slope * (rq - cn).astype(F32), MASKED)
        m = jnp.maximum(jnp.max(s, axis=1, keepdims=True), jnp.max(sn, axis=1, keepdims=True))
        e = jnp.exp(s - m)
        en = jnp.exp(sn - m)
        den = jnp.sum(e, axis=1, keepdims=True) + jnp.sum(en, axis=1, keepdims=True)
        probs.append(((e / den).astype(BF16), (en / den).astype(BF16)))
        lses.append(m + jnp.log(den))
    outs = [lax.dot_general(p, buf[0, 1, 0].astype(BF16), _NT, preferred_element_type=F32)
            + jnp.dot(pn, vn_ref[0, gi, 0], preferred_element_type=F32)
            for gi, ((_, buf, _), (p, pn)) in enumerate(zip(groups, probs))]
    mx = functools.reduce(jnp.maximum, lses)
    es = [jnp.exp(l - mx) for l in lses]
    tot = functools.reduce(lambda a, b: a + b, es)
    o_ref[0, 0] = functools.reduce(lambda a, b: a + b, [(e / tot) * o for e, o in zip(es, outs)])


def _dil_step(qkv, kv_new, bufs, slopes, gh):
    db, tq, f3 = qkv.shape
    f = f3 // 3
    hd = HEAD_DIM
    ng = f // (gh * hd)
    split = lambda i: _pad_rows(qkv[:, :, i * f:(i + 1) * f].reshape(db, tq, ng * gh, hd)).reshape(db, ng, gh, QROWS, hd)
    q8, k8, v8 = split(0), split(1), split(2)
    nf = jnp.transpose(kv_new.reshape(db, tq, 2, ng, gh, hd), (0, 2, 3, 4, 1, 5))
    nf = jnp.pad(nf, ((0, 0),) * 4 + ((0, QROWS - tq), (0, 0)))
    hblk = pl.BlockSpec((1, ng, 1, QROWS, hd), lambda b, h: (b, 0, h, 0, 0))
    views = [jnp.transpose(buf, (0, 2, 3, 4, 1)) for buf in bufs]
    bspecs = [pl.BlockSpec((1, 2, 1, hd, v.shape[-1]), lambda b, h: (b, 0, h, 0, 0)) for v in views]
    res = pl.pallas_call(
        functools.partial(_dil_step_kernel, gh=gh, nq=tq),
        grid=(db, gh),
        in_specs=[pl.BlockSpec(memory_space=pltpu.SMEM), hblk, hblk, hblk,
                  pl.BlockSpec((1, 2, ng, 1, QROWS, hd), lambda b, h: (b, 0, 0, h, 0, 0))] + bspecs,
        out_specs=[pl.BlockSpec((1, 1, QROWS, hd), lambda b, h: (b, h, 0, 0))] + bspecs,
        out_shape=[jax.ShapeDtypeStruct((db, gh, QROWS, hd), F32)]
        + [jax.ShapeDtypeStruct(v.shape, v.dtype) for v in views],
        compiler_params=_cparams(("arbitrary", "arbitrary")),
        name="dil_step_sample",
    )(slopes, q8, k8, v8, nf, *views)
    o = jnp.transpose(res[0][:, :, :tq], (0, 2, 1, 3)).reshape(db, tq, gh * hd)
    return o, [jnp.transpose(r, (0, 4, 1, 2, 3)) for r in res[1:]]


def _fox_sample_layer(x, g, w_in, b_f, w_out, layer, cache_kv, cache_lf, page_table, final_g=None):
    db, tq, d = x.shape
    x2 = x.reshape(db * tq, d)
    nh = b_f.shape[0]
    w = nh * HEAD_DIM
    qkv, kv, gate, lf = _proj(x2, g, w_in, w, w, b_f)
    lf3 = lf.reshape(db, tq, nh)
    o = _paged_attn("fox", layer, page_table, qkv.reshape(db, tq, 3 * w), cache_kv, nh, cache_lf=cache_lf, lf_new=lf3)
    y = _gated_out(x2, [o.reshape(db * tq, w)], gate, w_out, final_g=final_g)
    return y.reshape(db, tq, d), kv.reshape(db, tq, 2, nh, HEAD_DIM), lf3


def _dil_sample_layer(x, g, w_in, w_out, bufs):
    db, tq, d = x.shape
    x2 = x.reshape(db * tq, d)
    ng = len(DIL_PATTERNS)
    wg = w_out.shape[0]
    gh = wg // HEAD_DIM
    wq = ng * wg
    qkv, kv, gate = _proj(x2, g, w_in, wq, wg)
    o, new_bufs = _dil_step(qkv.reshape(db, tq, 3 * wq), kv.reshape(db, tq, 2 * wq), bufs, _alibi_slopes(ng * gh), gh)
    y = _gated_out(x2, [o.reshape(db * tq, wg)], gate, w_out)
    return y.reshape(db, tq, d), new_bufs


def _moba_sample_layer(x, g, w_in, w_out, layer, cache_kv, page_table):
    db, tq, d = x.shape
    x2 = x.reshape(db * tq, d)
    w = w_out.shape[0]
    nh = w // HEAD_DIM
    qkv, kv, gate = _proj(x2, g, w_in, w, w)
    o = _paged_attn("moba", layer, page_table, qkv.reshape(db, tq, 3 * w), cache_kv, nh, slopes=_alibi_slopes(nh))
    y = _gated_out(x2, [o.reshape(db * tq, w)], gate, w_out)
    return y.reshape(db, tq, d), kv.reshape(db, tq, 2, nh, HEAD_DIM)


def kernel(x_prompt, x_sample, cache_fox_kv, cache_fox_logf, cache_dil_kv0, cache_dil_kv1, cache_dil_kv2, cache_moba_kv, page_table, fox_norm, fox_w_in, fox_b_f, fox_w_out, dil_norm, dil_w_in, dil_w_out, moba_norm, moba_w_in, moba_w_out, final_norm):
    depth = fox_norm.shape[0] + dil_norm.shape[0] + moba_norm.shape[0]
    dil_bufs = (cache_dil_kv0, cache_dil_kv1, cache_dil_kv2)
    xp, xs = x_prompt, x_sample
    fox_kv_p, fox_kv_s, fox_lf_p, fox_lf_s, moba_p, moba_s = [], [], [], [], [], []
    dil_p = [[] for _ in dil_bufs]
    dil_s = [[] for _ in dil_bufs]
    for i in range(depth):
        kind, j = i % 3, i // 3
        fg = final_norm if i == depth - 1 else None
        if kind == 0:
            xp, kv, lf = _fox_prompt_layer(xp, fox_norm[j], fox_w_in[j], fox_b_f[j], fox_w_out[j], fg)
            fox_kv_p.append(kv)
            fox_lf_p.append(lf)
            xs, kv, lf = _fox_sample_layer(xs, fox_norm[j], fox_w_in[j], fox_b_f[j], fox_w_out[j], j,
                                           cache_fox_kv, cache_fox_logf, page_table, fg)
            fox_kv_s.append(kv)
            fox_lf_s.append(lf)
        elif kind == 1:
            xp, caches = _dil_prompt_layer(xp, dil_norm[j], dil_w_in[j], dil_w_out[j])
            xs, bufs = _dil_sample_layer(xs, dil_norm[j], dil_w_in[j], dil_w_out[j], [b[j] for b in dil_bufs])
            for gi in range(len(dil_bufs)):
                dil_p[gi].append(caches[gi])
                dil_s[gi].append(bufs[gi])
        else:
            xp, kv = _moba_prompt_layer(xp, moba_norm[j], moba_w_in[j], moba_w_out[j])
            moba_p.append(kv)
            xs, kv = _moba_sample_layer(xs, moba_norm[j], moba_w_in[j], moba_w_out[j], j, cache_moba_kv, page_table)
            moba_s.append(kv)
    if depth % 3 != 1:
        raise NotImplementedError("final RMSNorm is fused into the last forgetting-attention layer")
    return (xp, xs,
            jnp.stack(fox_kv_p), jnp.stack(fox_kv_s), jnp.stack(fox_lf_p), jnp.stack(fox_lf_s),
            jnp.stack(dil_p[0]), jnp.stack(dil_s[0]), jnp.stack(dil_p[1]), jnp.stack(dil_s[1]),
            jnp.stack(dil_p[2]), jnp.stack(dil_s[2]),
            jnp.stack(moba_p), jnp.stack(moba_s))
```

```python
import functools

import jax
import jax.numpy as jnp
from jax import lax
from jax.experimental import pallas as pl
from jax.experimental.pallas import tpu as pltpu

F32 = jnp.float32
BF16 = jnp.bfloat16

HEAD_DIM = 64
LANES = 128
RMS_EPS = 1e-6
SCALE = HEAD_DIM ** -0.5
ALIBI_MAX = 8.0
MASKED = -1e30
VMEM_LIMIT = 56 * 1024 * 1024

DIL_PATTERNS = ((128, 1), (512, 4), (2048, 16))
MOBA_BLOCK = 256
MOBA_TOPK = 3
FLASH_TILE = 256

_NT = (((1,), (1,)), ((), ()))


def _cparams(sem):
    return pltpu.CompilerParams(dimension_semantics=sem, vmem_limit_bytes=VMEM_LIMIT)


def _alibi_slopes(n):
    return jnp.exp2(-ALIBI_MAX * jnp.arange(1, n + 1, dtype=F32) / n)


def _rms_bf16(x, g):
    ms = jnp.mean(x * x, axis=-1, keepdims=True)
    return ((x * lax.rsqrt(ms + RMS_EPS)) * g).astype(BF16)


def _split3(x):
    hi = x.astype(BF16)
    r1 = x - hi.astype(F32)
    mid = r1.astype(BF16)
    lo = (r1 - mid.astype(F32)).astype(BF16)
    return hi, mid, lo


def _dot3(x, m01):
    hi, mid, lo = _split3(x)
    d = lambda a: jnp.dot(a, m01, preferred_element_type=F32)
    return d(hi) + d(mid) + d(lo)


def _head_mask(lane, e):
    return (lane >= HEAD_DIM * e) & (lane < HEAD_DIM * (e + 1))


PROJ_CHUNK = 512


def _proj_kernel(x_ref, g_ref, w_ref, *rest, wq, wg, forget, feature_major, block_means):
    rest = list(rest)
    wt_ref = rest.pop(0) if feature_major else None
    if forget:
        wf_ref, bf_ref = rest.pop(0), rest.pop(0)
    qkv_ref, kv_ref, gate_ref = rest.pop(0), rest.pop(0), rest.pop(0)
    lf_ref = rest.pop(0) if forget else None
    km_ref = rest.pop(0) if block_means else None
    y = _rms_bf16(x_ref[...], g_ref[...])
    for c0 in range(0, 3 * wq + wg, PROJ_CHUNK):
        t = jnp.dot(y, w_ref[:, c0:c0 + PROJ_CHUNK], preferred_element_type=F32)
        if c0 < wq:
            qkv_ref[:, c0:c0 + PROJ_CHUNK] = (t * SCALE).astype(BF16)
        elif c0 < 3 * wq:
            qkv_ref[:, c0:c0 + PROJ_CHUNK] = t.astype(BF16)
            ck = slice(c0 - wq, c0 - wq + PROJ_CHUNK)
            if feature_major:
                kv_ref[0, ck, :] = lax.dot_general(wt_ref[ck, :], y, _NT, preferred_element_type=F32)
            else:
                kv_ref[:, ck] = t
            if block_means and c0 < 2 * wq:
                km_ref[0, :, ck] = jnp.mean(t.reshape(-1, MOBA_BLOCK, PROJ_CHUNK), axis=1)
        else:
            gate_ref[:, c0 - 3 * wq:c0 - 3 * wq + PROJ_CHUNK] = t
    if forget:
        fl = jnp.dot(y, wf_ref[...], preferred_element_type=F32)
        lf_ref[...] = jax.nn.log_sigmoid(fl + bf_ref[...])


def _proj(x2, g, w_in, wq, wg, b_f=None, seq=None, block_means=False):
    n, d = x2.shape
    wmain = 3 * wq + wg
    tm = min(512, n)
    wb = w_in.astype(BF16)
    row = lambda i: (i, 0)
    fixed = lambda i: (0, 0)
    once = dict(pipeline_mode=pl.Buffered(1))
    in_specs = [pl.BlockSpec((tm, d), row), pl.BlockSpec((1, d), fixed), pl.BlockSpec((d, wmain), fixed, **once)]
    args = [x2, g.reshape(1, d), wb[:, :wmain]]
    out_specs = [pl.BlockSpec((tm, 3 * wq), row)]
    out_shape = [jax.ShapeDtypeStruct((n, 3 * wq), BF16)]
    if seq is not None:
        spb = seq // tm
        in_specs.append(pl.BlockSpec((2 * wq, d), fixed, **once))
        args.append(jnp.transpose(wb[:, wq:3 * wq]))
        out_specs.append(pl.BlockSpec((1, 2 * wq, tm), lambda i: (i // spb, 0, i % spb)))
        out_shape.append(jax.ShapeDtypeStruct((n // seq, 2 * wq, seq), F32))
    else:
        out_specs.append(pl.BlockSpec((tm, 2 * wq), row))
        out_shape.append(jax.ShapeDtypeStruct((n, 2 * wq), F32))
    out_specs.append(pl.BlockSpec((tm, wg), row))
    out_shape.append(jax.ShapeDtypeStruct((n, wg), F32))
    if b_f is not None:
        nh = b_f.shape[0]
        in_specs += [pl.BlockSpec((d, nh), fixed), pl.BlockSpec((1, nh), fixed)]
        args += [wb[:, wmain:], b_f.reshape(1, nh)]
        out_specs.append(pl.BlockSpec((tm, nh), row))
        out_shape.append(jax.ShapeDtypeStruct((n, nh), F32))
    if block_means:
        out_specs.append(pl.BlockSpec((1, tm // MOBA_BLOCK, wq), lambda i: (i, 0, 0)))
        out_shape.append(jax.ShapeDtypeStruct((n // tm, tm // MOBA_BLOCK, wq), F32))
    return pl.pallas_call(
        functools.partial(_proj_kernel, wq=wq, wg=wg, forget=b_f is not None, feature_major=seq is not None,
                          block_means=block_means),
        grid=(n // tm,),
        in_specs=in_specs, out_specs=out_specs, out_shape=out_shape,
        compiler_params=_cparams(("arbitrary",)),
        name="norm_proj",
    )(*args)


def _out_kernel(x_ref, gate_ref, w_ref, *rest, n_o, final):
    o_refs = rest[:n_o]
    rest = rest[n_o:]
    if n_o == 1:
        o = o_refs[0][...]
    else:
        lse_refs, rest = rest[:n_o], rest[n_o:]
        lses = [r[...] for r in lse_refs]
        m = functools.reduce(jnp.maximum, lses)
        es = [jnp.exp(l - m) for l in lses]
        den = functools.reduce(lambda a, b: a + b, es)
        o = functools.reduce(lambda a, b: a + b, [(e / den) * r[...] for e, r in zip(es, o_refs)])
    h = (o * jax.nn.silu(gate_ref[...])).astype(BF16)
    y = x_ref[...] + jnp.dot(h, w_ref[...], preferred_element_type=F32)
    if final:
        fg_ref, y_ref = rest
        ms = jnp.mean(y * y, axis=-1, keepdims=True)
        y_ref[...] = (y * lax.rsqrt(ms + RMS_EPS)) * fg_ref[...]
    else:
        rest[0][...] = y


def _gated_out(x2, os_, gate2, w_out, lses=(), final_g=None):
    n, d = x2.shape
    w = gate2.shape[1]
    tm = min(512, n)
    row = lambda i: (i, 0)
    fixed = lambda i: (0, 0)
    in_specs = [pl.BlockSpec((tm, d), row), pl.BlockSpec((tm, w), row), pl.BlockSpec((w, d), fixed)]
    in_specs += [pl.BlockSpec((tm, w), row)] * (len(os_) + len(lses))
    args = [x2, gate2, w_out.astype(BF16), *os_, *lses]
    if final_g is not None:
        in_specs.append(pl.BlockSpec((1, d), fixed))
        args.append(final_g.reshape(1, d))
    return pl.pallas_call(
        functools.partial(_out_kernel, n_o=len(os_), final=final_g is not None),
        grid=(n // tm,),
        in_specs=in_specs,
        out_specs=pl.BlockSpec((tm, d), row),
        out_shape=jax.ShapeDtypeStruct((n, d), F32),
        compiler_params=_cparams(("arbitrary",)),
        name="gated_out",
    )(*args)


def _place3(parts, nh):
    rr = lax.broadcasted_iota(jnp.int32, (nh, LANES), 0)
    ll = lax.broadcasted_iota(jnp.int32, (nh, LANES), 1)
    return sum(jnp.dot(p, (ll == 3 * rr + k).astype(BF16), preferred_element_type=F32) for k, p in enumerate(parts))


def _fox_keybias_kernel(lf_ref, o_ref, carry, *, cb, nh):
    @pl.when(pl.program_id(1) == 0)
    def _():
        carry[...] = jnp.zeros_like(carry)

    r = lax.broadcasted_iota(jnp.int32, (cb, cb), 0)
    c = lax.broadcasted_iota(jnp.int32, (cb, cb), 1)
    lower = (c <= r).astype(BF16)
    cs = sum(jnp.dot(lower, p, preferred_element_type=F32) for p in _split3(lf_ref[0])) + carry[...]
    carry[...] = cs[cb - 1:cb, :]
    o_ref[0] = _place3(_split3(-cs), nh).astype(BF16)


def _fox_keybias(lf):
    b, s, nh = lf.shape
    cb = min(512, s)
    return pl.pallas_call(
        functools.partial(_fox_keybias_kernel, cb=cb, nh=nh),
        grid=(b, s // cb),
        in_specs=[pl.BlockSpec((1, cb, nh), lambda i, j: (i, j, 0))],
        out_specs=pl.BlockSpec((1, cb, LANES), lambda i, j: (i, j, 0)),
        out_shape=jax.ShapeDtypeStruct((b, s, LANES), BF16),
        scratch_shapes=[pltpu.VMEM((1, nh), F32)],
        compiler_params=_cparams(("arbitrary", "arbitrary")),
        name="fox_keybias",
    )(lf)


def _moba_keyrows_kernel(slope_ref, o_ref, *, cb, nh):
    base = pl.program_id(0) * cb
    pos = base + lax.broadcasted_iota(jnp.int32, (cb, nh), 0)
    placed = _place3(_split3(pos.astype(F32) * slope_ref[...]), nh)
    posl = base + lax.broadcasted_iota(jnp.int32, (cb, LANES), 0)
    lane = lax.broadcasted_iota(jnp.int32, (cb, LANES), 1)
    onehot = jnp.where(lane - LANES // 2 == posl // MOBA_BLOCK, 1.0, 0.0)
    o_ref[0] = (placed + onehot).astype(BF16)


def _moba_keyrows(s, slopes):
    nh = slopes.shape[0]
    cb = min(512, s)
    return pl.pallas_call(
        functools.partial(_moba_keyrows_kernel, cb=cb, nh=nh),
        grid=(s // cb,),
        in_specs=[pl.BlockSpec((1, nh), lambda j: (0, 0))],
        out_specs=pl.BlockSpec((1, cb, LANES), lambda j: (0, j, 0)),
        out_shape=jax.ShapeDtypeStruct((1, s, LANES), BF16),
        compiler_params=_cparams(("arbitrary",)),
        name="moba_keyrows",
    )(slopes.reshape(1, nh))


def _flash_kernel(*refs, t, nt, masked_blocks):
    if masked_blocks:
        q_ref, k_ref, kx_ref, v_ref, sel_ref, o_ref, vt_sc, rhs_sc = refs
    else:
        q_ref, k_ref, kx_ref, v_ref, o_ref, vt_sc, rhs_sc = refs
    hp = pl.program_id(1)
    qi = pl.program_id(2)
    half = LANES // 2

    @pl.when(qi == 0)
    def _():
        for j in range(nt):
            vt_sc[j] = jnp.transpose(v_ref[0, j * t:(j + 1) * t, :].astype(F32)).astype(BF16)

    q_t = jnp.transpose(q_ref[0].astype(F32))
    frow = lax.broadcasted_iota(jnp.int32, (LANES, t), 0)
    xrow = lax.broadcasted_iota(jnp.int32, (half, t), 0)
    for e in range(2):
        h = 2 * hp + e
        qm = jnp.where(_head_mask(frow, e), q_t, 0.0).astype(BF16)
        ones = jnp.where((xrow >= 3 * h) & (xrow < 3 * h + 3), 1.0, 0.0).astype(BF16)
        low = sel_ref[0, 0, e] if masked_blocks else jnp.zeros((half, t), BF16)
        rhs_sc[e] = jnp.concatenate([qm, ones, low], axis=0)
    krow = lax.broadcasted_iota(jnp.int32, (t, t), 0)
    qcol = lax.broadcasted_iota(jnp.int32, (t, t), 1)

    def scores(j):
        ks = pl.multiple_of(j * t, t)
        kc = jnp.concatenate([k_ref[0, pl.ds(ks, t), :], kx_ref[0, pl.ds(ks, t), :]], axis=1)
        return tuple(jnp.dot(kc, rhs_sc[e], preferred_element_type=F32) for e in range(2))

    def update(j, s2, state, diagonal):
        vt = vt_sc[j]
        new = []
        for e in range(2):
            m_prev, l_prev, acc = state[e]
            s = jnp.where(krow <= qcol, s2[e], MASKED) if diagonal else s2[e]
            m_new = jnp.maximum(m_prev, jnp.max(s, axis=0, keepdims=True))
            alpha = jnp.exp(m_prev - m_new)
            p = jnp.exp(s - m_new)
            l_new = alpha * l_prev + jnp.sum(p, axis=0, keepdims=True)
            pv = jnp.dot(vt[e * half:(e + 1) * half, :], p.astype(BF16), preferred_element_type=F32)
            new.append((m_new, l_new, alpha * acc + pv))
        return tuple(new)

    def body(j, carry):
        s2, state = carry
        s_next = scores(j + 1)
        return s_next, update(j, s2, state, False)

    init = tuple((jnp.full((1, t), MASKED, F32), jnp.zeros((1, t), F32), jnp.zeros((half, t), F32))
                 for _ in range(2))
    s2, state = lax.fori_loop(0, qi, body, (scores(0), init))
    state = update(qi, s2, state, True)
    o_t = jnp.concatenate([acc / l for _, l, acc in state], axis=0)
    o_ref[0] = jnp.transpose(o_t)


def _flash_prompt(qkv, keyrows, t, sel=None, name="attn_prompt"):
    b, s, w3 = qkv.shape
    w = w3 // 3
    hp = w // LANES
    nt = s // t
    per_batch = keyrows.shape[0] > 1
    in_specs = [pl.BlockSpec((1, t, LANES), lambda i, h, j: (i, j, h)),
                pl.BlockSpec((1, s, LANES), lambda i, h, j: (i, 0, hp + h)),
                pl.BlockSpec((1, s, LANES), lambda i, h, j: (i if per_batch else 0, 0, 0)),
                pl.BlockSpec((1, s, LANES), lambda i, h, j: (i, 0, 2 * hp + h))]
    args = [qkv, qkv, keyrows, qkv]
    if sel is not None:
        in_specs.append(pl.BlockSpec((1, 1, 2, LANES // 2, t), lambda i, h, j: (i, h, 0, 0, j)))
        args.append(sel)
    return pl.pallas_call(
        functools.partial(_flash_kernel, t=t, nt=nt, masked_blocks=sel is not None),
        grid=(b, hp, nt),
        in_specs=in_specs,
        out_specs=pl.BlockSpec((1, t, LANES), lambda i, h, j: (i, j, h)),
        out_shape=jax.ShapeDtypeStruct((b, s, w), F32),
        scratch_shapes=[pltpu.VMEM((nt, LANES, t), BF16), pltpu.VMEM((2, 2 * LANES, t), BF16)],
        compiler_params=_cparams(("arbitrary", "arbitrary", "arbitrary")),
        name=name,
    )(*args)


def _topk_rank_mask(g, n_idx, own):
    nb = g.shape[0]
    cnt = jnp.zeros(g.shape, jnp.int32)
    for m in range(nb):
        gm = g[m:m + 1, :]
        beats = (gm > g) | ((gm == g) & (m < n_idx))
        cnt = cnt + jnp.where(beats & (m < own), 1, 0)
    return (n_idx < own) & (cnt < MOBA_TOPK) & (jnp.abs(g) < jnp.inf)


def _moba_select_kernel(q_ref, km_ref, o_ref, *, t, nb):
    qi = pl.program_id(2)
    q = q_ref[0]
    km = km_ref[0].astype(BF16)
    klane = lax.broadcasted_iota(jnp.int32, (nb, LANES), 1)
    n_idx = lax.broadcasted_iota(jnp.int32, (nb, t), 0)
    own = (qi * t + lax.broadcasted_iota(jnp.int32, (1, t), 1)) // MOBA_BLOCK
    for e in range(2):
        kme = jnp.where(_head_mask(klane, e), km, jnp.zeros_like(km))
        g = lax.dot_general(kme, q, _NT, preferred_element_type=F32)
        keep = _topk_rank_mask(g, n_idx, own) | (n_idx == own)
        bias = jnp.where(keep, 0.0, MASKED)
        if nb < LANES // 2:
            bias = jnp.concatenate([bias, jnp.zeros((LANES // 2 - nb, t), F32)], axis=0)
        o_ref[0, 0, e] = bias.astype(BF16)


def _moba_select(qkv, kmean, t):
    b, s, w3 = qkv.shape
    w = w3 // 3
    hp = w // LANES
    nb = kmean.shape[1]
    return pl.pallas_call(
        functools.partial(_moba_select_kernel, t=t, nb=nb),
        grid=(b, hp, s // t),
        in_specs=[pl.BlockSpec((1, t, LANES), lambda i, h, j: (i, j, h)),
                  pl.BlockSpec((1, nb, LANES), lambda i, h, j: (i, 0, h))],
        out_specs=pl.BlockSpec((1, 1, 2, LANES // 2, t), lambda i, h, j: (i, h, 0, 0, j)),
        out_shape=jax.ShapeDtypeStruct((b, hp, 2, LANES // 2, s), BF16),
        compiler_params=_cparams(("arbitrary", "arbitrary", "arbitrary")),
        name="moba_select",
    )(qkv, kmean)


def _dil_attn_kernel(slope_ref, q_ref, kp_ref, kc_ref, vp_ref, vc_ref, o_ref, lse_ref, *, t, d, g, gh):
    ut = pl.program_id(2)
    lane = lax.broadcasted_iota(jnp.int32, (t, LANES), 1)
    row = lax.broadcasted_iota(jnp.int32, (t, t), 0)
    col = lax.broadcasted_iota(jnp.int32, (t, t), 1)
    jc = row - col
    jp = jc + t
    ok_c = jc >= 0
    ok_p = (jp <= t) & (ut > 0)
    dist_c = (jc * d).astype(F32)
    dist_p = (jp * d).astype(F32)
    heads = [(pair, e) for pair in range(gh // 2) for e in range(2)]
    pair_lanes = lambda pair: slice(pair * LANES, (pair + 1) * LANES)
    raw = []
    for pair, e in heads:
        sl = pair_lanes(pair)
        q = q_ref[0, :, sl]
        qm = jnp.where(_head_mask(lane, e), q, jnp.zeros_like(q))
        raw.append((lax.dot_general(qm, kp_ref[0, :, sl], _NT, preferred_element_type=F32),
                    lax.dot_general(qm, kc_ref[0, :, sl], _NT, preferred_element_type=F32)))
    probs, lses = [], []
    for (pair, e), (sp, sc) in zip(heads, raw):
        slope = slope_ref[g * gh + 2 * pair + e]
        sp = jnp.where(ok_p, sp - slope * dist_p, MASKED)
        sc = jnp.where(ok_c, sc - slope * dist_c, MASKED)
        m = jnp.maximum(jnp.max(sp, axis=1, keepdims=True), jnp.max(sc, axis=1, keepdims=True))
        ep = jnp.exp(sp - m)
        ec = jnp.exp(sc - m)
        den = jnp.sum(ep, axis=1, keepdims=True) + jnp.sum(ec, axis=1, keepdims=True)
        probs.append(((ep / den).astype(BF16), (ec / den).astype(BF16)))
        lses.append(m + jnp.log(den))
    outs = [jnp.dot(pp, vp_ref[0, :, pair_lanes(pair)], preferred_element_type=F32)
            + jnp.dot(pc, vc_ref[0, :, pair_lanes(pair)], preferred_element_type=F32)
            for (pair, e), (pp, pc) in zip(heads, probs)]
    for pair in range(gh // 2):
        sl = pair_lanes(pair)
        o_ref[0, :, sl] = jnp.where(lane < HEAD_DIM, outs[2 * pair], outs[2 * pair + 1])
        lse_ref[0, :, sl] = jnp.where(lane < HEAD_DIM, lses[2 * pair], lses[2 * pair + 1])


def _dil_attn_prompt(qkv, slopes, g, window, d, gh):
    b, s, f3 = qkv.shape
    gw = gh * HEAD_DIM
    nblk = f3 // gw
    per = nblk // 3
    t = window // d
    su = s // d
    qv = qkv.reshape(b, su, d * f3)
    cur = lambda sec: (lambda i, r, u: (i, u, r * nblk + sec * per + g))
    prev = lambda sec: (lambda i, r, u: (i, jnp.maximum(u - 1, 0), r * nblk + sec * per + g))
    blk = (1, t, gw)
    o, lse = pl.pallas_call(
        functools.partial(_dil_attn_kernel, t=t, d=d, g=g, gh=gh),
        grid=(b, d, su // t),
        in_specs=[pl.BlockSpec(memory_space=pltpu.SMEM),
                  pl.BlockSpec(blk, cur(0)), pl.BlockSpec(blk, prev(1)), pl.BlockSpec(blk, cur(1)),
                  pl.BlockSpec(blk, prev(2)), pl.BlockSpec(blk, cur(2))],
        out_specs=[pl.BlockSpec(blk, lambda i, r, u: (i, u, r))] * 2,
        out_shape=[jax.ShapeDtypeStruct((b, su, d * gw), F32)] * 2,
        compiler_params=_cparams(("arbitrary", "arbitrary", "arbitrary")),
        name="dil_attn_prompt",
    )(slopes, qv, qv, qv, qv, qv)
    return o.reshape(b, s, gw), lse.reshape(b, s, gw)


def _fox_prompt_layer(x, g, w_in, b_f, w_out, final_g=None):
    b, s, d = x.shape
    x2 = x.reshape(b * s, d)
    nh = b_f.shape[0]
    w = nh * HEAD_DIM
    qkv, kv_t, gate, lf = _proj(x2, g, w_in, w, w, b_f, seq=s)
    o = _flash_prompt(qkv.reshape(b, s, 3 * w), _fox_keybias(lf.reshape(b, s, nh)), min(FLASH_TILE, s),
                      name="fox_attn_prompt")
    y = _gated_out(x2, [o.reshape(b * s, w)], gate, w_out, final_g=final_g)
    return y.reshape(b, s, d), _token_major(kv_t, nh), lf.reshape(b, s, nh)


def _dil_prompt_layer(x, g, w_in, w_out):
    b, s, d = x.shape
    x2 = x.reshape(b * s, d)
    ng = len(DIL_PATTERNS)
    wg = w_out.shape[0]
    gh = wg // HEAD_DIM
    wq = ng * wg
    qkv, kv_t, gate = _proj(x2, g, w_in, wq, wg, seq=s)
    slopes = _alibi_slopes(ng * gh)
    qkv3 = qkv.reshape(b, s, 3 * wq)
    os_, lses = [], []
    for gi, (window, dil) in enumerate(DIL_PATTERNS):
        o, lse = _dil_attn_prompt(qkv3, slopes, gi, window, dil, gh)
        os_.append(o.reshape(b * s, wg))
        lses.append(lse.reshape(b * s, wg))
    y = _gated_out(x2, os_, gate, w_out, lses=lses)
    kv6 = kv_t.reshape(b, 2, ng, gh * HEAD_DIM, s)
    caches = [_token_major(kv6[:, :, gi, :, s - min(window, s):].reshape(b, 2 * gh * HEAD_DIM, -1), gh)
              for gi, (window, _) in enumerate(DIL_PATTERNS)]
    return y.reshape(b, s, d), caches


def _token_major(kv_t, nh):
    b, _, length = kv_t.shape
    return jnp.transpose(kv_t.reshape(b, 2, nh, HEAD_DIM, length), (0, 4, 1, 2, 3))


def _moba_prompt_layer(x, g, w_in, w_out):
    b, s, d = x.shape
    x2 = x.reshape(b * s, d)
    w = w_out.shape[0]
    nh = w // HEAD_DIM
    qkv, kv_t, gate, kmean = _proj(x2, g, w_in, w, w, seq=s, block_means=True)
    qkv3 = qkv.reshape(b, s, 3 * w)
    t = MOBA_BLOCK
    sel = _moba_select(qkv3, kmean.reshape(b, s // MOBA_BLOCK, w), t)
    o = _flash_prompt(qkv3, _moba_keyrows(s, _alibi_slopes(nh)), t, sel=sel, name="moba_attn_prompt")
    y = _gated_out(x2, [o.reshape(b * s, w)], gate, w_out)
    return y.reshape(b, s, d), _token_major(kv_t, nh)


QROWS = 8


PAGES_PER_STEP = 4


def _paged_kernel(pt_ref, *refs, mode, nh, npg, pg, nq):
    del pt_ref
    pps = PAGES_PER_STEP
    if mode == "fox":
        q_ref, kv_refs, lf_refs = refs[0], refs[1:1 + pps], refs[1 + pps:1 + 2 * pps]
        kn_ref, vn_ref, lfn_ref, o_ref, m_sc, l_sc, acc_sc, carry = refs[1 + 2 * pps:]
    else:
        slope_ref, q_ref, kv_refs = refs[0], refs[1], refs[2:2 + pps]
        kn_ref, vn_ref, o_ref, m_sc, l_sc, acc_sc, ksum = refs[2 + pps:]
    step = pl.program_id(1)
    past = npg * pg
    ppb = MOBA_BLOCK // pg
    if mode == "fox":
        @pl.when(step == 0)
        def _():
            carry[...] = jnp.zeros_like(carry)

        r = lax.broadcasted_iota(jnp.int32, (pg, pg), 0)
        c = lax.broadcasted_iota(jnp.int32, (pg, pg), 1)
        upper = (r <= c).astype(BF16)
        cs = []
        run = carry[...]
        for lf_ref in lf_refs:
            cs.append(_dot3(lf_ref[...], upper) + run)
            run = cs[-1][:, pg - 1:pg]
        carry[...] = run
    else:
        rowq = lax.broadcasted_iota(jnp.int32, (QROWS, pg), 0)
        colk = lax.broadcasted_iota(jnp.int32, (QROWS, pg), 1)
        for blk in range(pps // ppb):
            for h in range(nh):
                ksum[h, step * (pps // ppb) + blk] = functools.reduce(
                    lambda a, b: a + b, [kv_ref[0, h] for kv_ref in kv_refs[blk * ppb:(blk + 1) * ppb]])

    scores = []
    for i, kv_ref in enumerate(kv_refs):
        if mode == "moba":
            dist = (past + rowq - ((step * pps + i) * pg + colk)).astype(F32)
        for h in range(nh):
            s = jnp.dot(q_ref[0, h], kv_ref[0, h].astype(BF16), preferred_element_type=F32)
            if mode == "fox":
                scores.append(s - cs[i][h:h + 1, :])
            else:
                scores.append(s - slope_ref[h] * dist)
    s = jnp.concatenate(scores, axis=0)
    m = jnp.max(s, axis=1, keepdims=True)
    e = jnp.exp(s - m)
    l = jnp.sum(e, axis=1, keepdims=True)
    for i, kv_ref in enumerate(kv_refs):
        p = step * pps + i
        for h in range(nh):
            rows = slice((i * nh + h) * QROWS, (i * nh + h + 1) * QROWS)
            m_sc[h, p] = m[rows]
            l_sc[h, p] = l[rows]
            acc_sc[h, p] = lax.dot_general(e[rows].astype(BF16), kv_ref[1, h].astype(BF16), _NT,
                                           preferred_element_type=F32)

    @pl.when(step == npg // pps - 1)
    def _():
        rq = lax.broadcasted_iota(jnp.int32, (QROWS, QROWS), 0)
        cn = lax.broadcasted_iota(jnp.int32, (QROWS, QROWS), 1)
        new_ok = (cn <= rq) & (cn < nq)
        if mode == "fox":
            up8 = (rq <= cn).astype(BF16)
            csn = _dot3(lfn_ref[0], up8) + carry[...]
        scores_n = []
        for h in range(nh):
            sn = lax.dot_general(q_ref[0, h], kn_ref[0, h], _NT, preferred_element_type=F32)
            if mode == "fox":
                sn = sn - csn[h:h + 1, :]
            else:
                sn = sn - slope_ref[h] * (rq - cn).astype(F32)
            scores_n.append(jnp.where(new_ok, sn, MASKED))
        sn = jnp.concatenate(scores_n, axis=0)
        mn = jnp.max(sn, axis=1, keepdims=True)
        en = jnp.exp(sn - mn)
        ln = jnp.sum(en, axis=1, keepdims=True)
        if mode == "moba":
            nblk = npg // ppb
            km_lane = lax.broadcasted_iota(jnp.int32, (q_ref.shape[3], nblk), 1)
            gates = []
            for h in range(nh):
                kmean = jnp.zeros(km_lane.shape, F32)
                for n in range(nblk):
                    kmean = jnp.where(km_lane == n, jnp.sum(ksum[h, n], axis=1, keepdims=True), kmean)
                kmean = (kmean * (1.0 / MOBA_BLOCK)).astype(BF16)
                gates.append(jnp.dot(q_ref[0, h], kmean, preferred_element_type=F32))
            g = jnp.concatenate(gates, axis=0)
            blk_lane = lax.broadcasted_iota(jnp.int32, g.shape, 1)
            cnt = jnp.zeros(g.shape, jnp.int32)
            for m_ in range(nblk):
                gm = g[:, m_:m_ + 1]
                cnt = cnt + jnp.where((gm > g) | ((gm == g) & (m_ < blk_lane)), 1, 0)
            sel = (cnt < MOBA_TOPK) & (jnp.abs(g) < jnp.inf)
        for h in range(nh):
            rows = slice(h * QROWS, (h + 1) * QROWS)
            an = jnp.dot(en[rows].astype(BF16), vn_ref[0, h], preferred_element_type=F32)
            mp = m_sc[h]
            if mode == "moba":
                selp = jnp.stack([sel[rows, n // ppb:n // ppb + 1] for n in range(npg)], axis=0)
                mp = jnp.where(selp, mp, MASKED)
            mx = jnp.maximum(jnp.max(mp, axis=0), mn[rows])
            wp = jnp.exp(mp - mx)
            wn = jnp.exp(mn[rows] - mx)
            num = jnp.sum(wp * acc_sc[h], axis=0) + wn * an
            den = jnp.sum(wp * l_sc[h], axis=0) + wn * ln[rows]
            o_ref[0, h] = num / den


def _pad_rows(x):
    x = jnp.transpose(x, (0, 2, 1, 3))
    return jnp.pad(x, ((0, 0), (0, 0), (0, QROWS - x.shape[2]), (0, 0)))


def _paged_attn(mode, layer, page_table, qkv, cache_kv, nh, cache_lf=None, lf_new=None, slopes=None):
    db, tq, w3 = qkv.shape
    w = w3 // 3
    nl, nphys, pg = cache_kv.shape[:3]
    npg = page_table.shape[1]
    hd = HEAD_DIM
    split = lambda i: _pad_rows(qkv[:, :, i * w:(i + 1) * w].reshape(db, tq, nh, hd))
    q8, k8, v8 = split(0), split(1), split(2)
    pages = jnp.transpose(cache_kv, (0, 1, 3, 4, 5, 2))
    hblk = pl.BlockSpec((1, nh, QROWS, hd), lambda b, p, pt: (b, 0, 0, 0))
    pps = PAGES_PER_STEP
    assert npg % pps == 0 and (pg * pps) % MOBA_BLOCK == 0
    page = lambda i: (lambda b, p, pt: (layer, pt[b, p * pps + i], 0, 0, 0, 0))
    in_specs = [hblk] + [pl.BlockSpec((None, None, 2, nh, hd, pg), page(i)) for i in range(pps)]
    args = [q8] + [pages] * pps
    scratch = [pltpu.VMEM((nh, npg, QROWS, 1), F32), pltpu.VMEM((nh, npg, QROWS, 1), F32),
               pltpu.VMEM((nh, npg, QROWS, hd), F32)]
    if mode == "fox":
        lf_page = lambda i: (lambda b, p, pt: (layer, pt[b, p * pps + i], 0, 0))
        in_specs += [pl.BlockSpec((None, None, nh, pg), lf_page(i)) for i in range(pps)]
        args += [jnp.transpose(cache_lf, (0, 1, 3, 2))] * pps
    in_specs += [hblk, hblk]
    args += [k8, v8]
    if mode == "fox":
        lfn = jnp.pad(jnp.transpose(lf_new, (0, 2, 1)), ((0, 0), (0, 0), (0, QROWS - tq)))
        in_specs.append(pl.BlockSpec((1, nh, QROWS), lambda b, p, pt: (b, 0, 0)))
        args.append(lfn)
        scratch.append(pltpu.VMEM((nh, 1), F32))
    else:
        in_specs = [pl.BlockSpec(memory_space=pltpu.SMEM)] + in_specs
        args = [slopes] + args
        scratch.append(pltpu.VMEM((nh, npg * pg // MOBA_BLOCK, hd, pg), F32))
    o8 = pl.pallas_call(
        functools.partial(_paged_kernel, mode=mode, nh=nh, npg=npg, pg=pg, nq=tq),
        grid_spec=pltpu.PrefetchScalarGridSpec(
            num_scalar_prefetch=1, grid=(db, npg // pps), in_specs=in_specs, out_specs=hblk, scratch_shapes=scratch),
        out_shape=jax.ShapeDtypeStruct((db, nh, QROWS, hd), F32),
        compiler_params=_cparams(("arbitrary", "arbitrary")),
        name=mode + "_attn_sample",
    )(page_table, *args)
    return jnp.transpose(o8[:, :, :tq], (0, 2, 1, 3)).reshape(db, tq, w)


DIL_HEADS_PER_STEP = 2


def _dil_step_kernel(slope_ref, q_ref, kn_ref, vn_ref, nf_ref, b0_ref, b1_ref, b2_ref,
                     o_ref, n0_ref, n1_ref, n2_ref, *, gh, nq):
    rq = lax.broadcasted_iota(jnp.int32, (QROWS, QROWS), 0)
    cn = lax.broadcasted_iota(jnp.int32, (QROWS, QROWS), 1)
    jj = lax.broadcasted_iota(jnp.int32, (QROWS, LANES), 0)
    ll = lax.broadcasted_iota(jnp.int32, (QROWS, LANES), 1)
    place = ((ll == LANES - nq + jj) & (jj < nq)).astype(BF16)
    tail = lax.broadcasted_iota(jnp.int32, (q_ref.shape[-1], LANES), 1) >= LANES - nq
    tn = (((0,), (0,)), ((), ()))
    groups = list(zip(DIL_PATTERNS, (b0_ref, b1_ref, b2_ref), (n0_ref, n1_ref, n2_ref)))
    for hh in range(DIL_HEADS_PER_STEP):
        h = pl.program_id(1) * DIL_HEADS_PER_STEP + hh
        raw = [(jnp.dot(q_ref[0, gi, hh], buf[0, 0, hh].astype(BF16), preferred_element_type=F32),
                lax.dot_general(q_ref[0, gi, hh], kn_ref[0, gi, hh], _NT, preferred_element_type=F32))
               for gi, (_, buf, _) in enumerate(groups)]
        for gi, (_, buf, nbuf) in enumerate(groups):
            length = buf.shape[-1]
            for kv in range(2):
                rolled = pltpu.roll(buf[0, kv, hh], length - nq, axis=1)
                new_t = sum(lax.dot_general(a, place, tn, preferred_element_type=F32)
                            for a in _split3(nf_ref[0, kv, gi, hh]))
                if length > LANES:
                    nbuf[0, kv, hh, :, :length - LANES] = rolled[:, :length - LANES]
                nbuf[0, kv, hh, :, length - LANES:] = jnp.where(tail, new_t, rolled[:, length - LANES:])
        probs, lses = [], []
        for gi, ((window, d), buf, _) in enumerate(groups):
            length = buf.shape[-1]
            slope = slope_ref[gi * gh + h]
            s, sn = raw[gi]
            t_i = lax.broadcasted_iota(jnp.int32, (QROWS, length), 0)
            l_i = lax.broadcasted_iota(jnp.int32, (QROWS, length), 1)
            ok = (l_i >= t_i) & (((l_i - t_i) & (d - 1)) == 0)
            s = jnp.where(ok, s - slope * (length + t_i - l_i).astype(F32), MASKED)
            okn = (cn <= rq) & (((rq - cn) & (d - 1)) == 0) & (cn < nq)
            sn = jnp.where(okn, sn - slope * (rq - cn).astype(F32), MASKED)
            m = jnp.maximum(jnp.max(s, axis=1, keepdims=True), jnp.max(sn, axis=1, keepdims=True))
            e = jnp.exp(s - m)
            en = jnp.exp(sn - m)
            den = jnp.sum(e, axis=1, keepdims=True) + jnp.sum(en, axis=1, keepdims=True)
            probs.append(((e / den).astype(BF16), (en / den).astype(BF16)))
            lses.append(m + jnp.log(den))
        outs = [lax.dot_general(p, buf[0, 1, hh].astype(BF16), _NT, preferred_element_type=F32)
                + jnp.dot(pn, vn_ref[0, gi, hh], preferred_element_type=F32)
                for gi, ((_, buf, _), (p, pn)) in enumerate(zip(groups, probs))]
        mx = functools.reduce(jnp.maximum, lses)
        es = [jnp.exp(l - mx) for l in lses]
        tot = functools.reduce(lambda a, b: a + b, es)
        o_ref[0, hh] = functools.reduce(lambda a, b: a + b, [(e / tot) * o for e, o in zip(es, outs)])


def _dil_step(qkv, kv_new, bufs, slopes, gh):
    db, tq, f3 = qkv.shape
    f = f3 // 3
    hd = HEAD_DIM
    ng = f // (gh * hd)
    split = lambda i: _pad_rows(qkv[:, :, i * f:(i + 1) * f].reshape(db, tq, ng * gh, hd)).reshape(db, ng, gh, QROWS, hd)
    q8, k8, v8 = split(0), split(1), split(2)
    nf = jnp.transpose(kv_new.reshape(db, tq, 2, ng, gh, hd), (0, 2, 3, 4, 1, 5))
    nf = jnp.pad(nf, ((0, 0),) * 4 + ((0, QROWS - tq), (0, 0)))
    hps = DIL_HEADS_PER_STEP
    hblk = pl.BlockSpec((1, ng, hps, QROWS, hd), lambda b, h: (b, 0, h, 0, 0))
    views = [jnp.transpose(buf, (0, 2, 3, 4, 1)) for buf in bufs]
    bspecs = [pl.BlockSpec((1, 2, hps, hd, v.shape[-1]), lambda b, h: (b, 0, h, 0, 0)) for v in views]
    res = pl.pallas_call(
        functools.partial(_dil_step_kernel, gh=gh, nq=tq),
        grid=(db, gh // hps),
        in_specs=[pl.BlockSpec(memory_space=pltpu.SMEM), hblk, hblk, hblk,
                  pl.BlockSpec((1, 2, ng, hps, QROWS, hd), lambda b, h: (b, 0, 0, h, 0, 0))] + bspecs,
        out_specs=[pl.BlockSpec((1, hps, QROWS, hd), lambda b, h: (b, h, 0, 0))] + bspecs,
        out_shape=[jax.ShapeDtypeStruct((db, gh, QROWS, hd), F32)]
        + [jax.ShapeDtypeStruct(v.shape, v.dtype) for v in views],
        compiler_params=_cparams(("arbitrary", "arbitrary")),
        name="dil_step_sample",
    )(slopes, q8, k8, v8, nf, *views)
    o = jnp.transpose(res[0][:, :, :tq], (0, 2, 1, 3)).reshape(db, tq, gh * hd)
    return o, [jnp.transpose(r, (0, 4, 1, 2, 3)) for r in res[1:]]


def _fox_sample_layer(x, g, w_in, b_f, w_out, layer, cache_kv, cache_lf, page_table, final_g=None):
    db, tq, d = x.shape
    x2 = x.reshape(db * tq, d)
    nh = b_f.shape[0]
    w = nh * HEAD_DIM
    qkv, kv, gate, lf = _proj(x2, g, w_in, w, w, b_f)
    lf3 = lf.reshape(db, tq, nh)
    o = _paged_attn("fox", layer, page_table, qkv.reshape(db, tq, 3 * w), cache_kv, nh, cache_lf=cache_lf, lf_new=lf3)
    y = _gated_out(x2, [o.reshape(db * tq, w)], gate, w_out, final_g=final_g)
    return y.reshape(db, tq, d), kv.reshape(db, tq, 2, nh, HEAD_DIM), lf3


def _dil_sample_layer(x, g, w_in, w_out, bufs):
    db, tq, d = x.shape
    x2 = x.reshape(db * tq, d)
    ng = len(DIL_PATTERNS)
    wg = w_out.shape[0]
    gh = wg // HEAD_DIM
    wq = ng * wg
    qkv, kv, gate = _proj(x2, g, w_in, wq, wg)
    o, new_bufs = _dil_step(qkv.reshape(db, tq, 3 * wq), kv.reshape(db, tq, 2 * wq), bufs, _alibi_slopes(ng * gh), gh)
    y = _gated_out(x2, [o.reshape(db * tq, wg)], gate, w_out)
    return y.reshape(db, tq, d), new_bufs


def _moba_sample_layer(x, g, w_in, w_out, layer, cache_kv, page_table):
    db, tq, d = x.shape
    x2 = x.reshape(db * tq, d)
    w = w_out.shape[0]
    nh = w // HEAD_DIM
    qkv, kv, gate = _proj(x2, g, w_in, w, w)
    o = _paged_attn("moba", layer, page_table, qkv.reshape(db, tq, 3 * w), cache_kv, nh, slopes=_alibi_slopes(nh))
    y = _gated_out(x2, [o.reshape(db * tq, w)], gate, w_out)
    return y.reshape(db, tq, d), kv.reshape(db, tq, 2, nh, HEAD_DIM)


def kernel(x_prompt, x_sample, cache_fox_kv, cache_fox_logf, cache_dil_kv0, cache_dil_kv1, cache_dil_kv2, cache_moba_kv, page_table, fox_norm, fox_w_in, fox_b_f, fox_w_out, dil_norm, dil_w_in, dil_w_out, moba_norm, moba_w_in, moba_w_out, final_norm):
    depth = fox_norm.shape[0] + dil_norm.shape[0] + moba_norm.shape[0]
    dil_bufs = (cache_dil_kv0, cache_dil_kv1, cache_dil_kv2)
    xp, xs = x_prompt, x_sample
    fox_kv_p, fox_kv_s, fox_lf_p, fox_lf_s, moba_p, moba_s = [], [], [], [], [], []
    dil_p = [[] for _ in dil_bufs]
    dil_s = [[] for _ in dil_bufs]
    for i in range(depth):
        kind, j = i % 3, i // 3
        fg = final_norm if i == depth - 1 else None
        if kind == 0:
            xp, kv, lf = _fox_prompt_layer(xp, fox_norm[j], fox_w_in[j], fox_b_f[j], fox_w_out[j], fg)
            fox_kv_p.append(kv)
            fox_lf_p.append(lf)
            xs, kv, lf = _fox_sample_layer(xs, fox_norm[j], fox_w_in[j], fox_b_f[j], fox_w_out[j], j,
                                           cache_fox_kv, cache_fox_logf, page_table, fg)
            fox_kv_s.append(kv)
            fox_lf_s.append(lf)
        elif kind == 1:
            xp, caches = _dil_prompt_layer(xp, dil_norm[j], dil_w_in[j], dil_w_out[j])
            xs, bufs = _dil_sample_layer(xs, dil_norm[j], dil_w_in[j], dil_w_out[j], [b[j] for b in dil_bufs])
            for gi in range(len(dil_bufs)):
                dil_p[gi].append(caches[gi])
                dil_s[gi].append(bufs[gi])
        else:
            xp, kv = _moba_prompt_layer(xp, moba_norm[j], moba_w_in[j], moba_w_out[j])
            moba_p.append(kv)
            xs, kv = _moba_sample_layer(xs, moba_norm[j], moba_w_in[j], moba_w_out[j], j, cache_moba_kv, page_table)
            moba_s.append(kv)
    if depth % 3 != 1:
        raise NotImplementedError("final RMSNorm is fused into the last forgetting-attention layer")
    return (xp, xs,
            jnp.stack(fox_kv_p), jnp.stack(fox_kv_s), jnp.stack(fox_lf_p), jnp.stack(fox_lf_s),
            jnp.stack(dil_p[0]), jnp.stack(dil_s[0]), jnp.stack(dil_p[1]), jnp.stack(dil_s[1]),
            jnp.stack(dil_p[2]), jnp.stack(dil_s[2]),
            jnp.stack(moba_p), jnp.stack(moba_s))
```
